```python
import math
import jax, jax.numpy as jnp
from jax import lax
import numpy as np

D_MODEL = 2048
BATCH = 4
SEQ = 2048
DEPTH = 4
DEC_BATCH = 128
DEC_SEQ = 8
PAST_LEN = 16384
PAGE_SIZE = 128

CONV_DIM = D_MODEL
CONV_WIDTH = 3
GLA_HEADS = 4
GLA_DK_TOT = D_MODEL // 2
GLA_DV_TOT = D_MODEL
GLA_DK = GLA_DK_TOT // GLA_HEADS
GLA_DV = GLA_DV_TOT // GLA_HEADS
GLA_RANK = 16
GLA_TAU = 16.0
GLA_CHUNK = 64
NORM_EPS = 1e-6
SPLITS = (D_MODEL, D_MODEL, CONV_DIM, CONV_DIM, CONV_DIM, CONV_DIM,
          GLA_DK_TOT, GLA_DK_TOT, GLA_DV_TOT, GLA_DV_TOT, GLA_RANK)
IN_WIDTH = 2 * D_MODEL + 4 * CONV_DIM + 2 * GLA_DK_TOT + 2 * GLA_DV_TOT + GLA_RANK

kernel_name = "hybrid_shortconv_gla_step"


def rmsnorm(x, g):
    x32 = x.astype(jnp.float32)
    y = x32 * lax.rsqrt(jnp.mean(x32 * x32, axis=-1, keepdims=True) + NORM_EPS)
    return (y * g.astype(jnp.float32)).astype(x.dtype)


def gla_chunked(q, k, v, log_a, s0):
    B, L, H, K = q.shape
    V = v.shape[-1]
    c = math.gcd(L, GLA_CHUNK)
    n = L // c

    def blocks(t):
        return jnp.moveaxis(t.astype(jnp.float32).reshape(B, n, c, H, t.shape[-1]), 1, 0)

    mask = jnp.tril(jnp.ones((c, c), dtype=bool))[None, :, :, None, None]

    def step(S, inp):
        qc, kc, vc, ac = inp
        b = jnp.cumsum(ac, axis=1)
        diff = b[:, :, None] - b[:, None, :]
        decay = jnp.exp(jnp.where(mask, diff, -jnp.inf))
        scores = jnp.sum(qc[:, :, None] * kc[:, None, :] * decay, axis=-1)
        o_intra = jnp.einsum('btsh,bshv->bthv', scores, vc)
        o_inter = jnp.einsum('bthk,bhkv->bthv', qc * jnp.exp(b), S)
        b_last = b[:, -1]
        k_dec = kc * jnp.exp(b_last[:, None] - b)
        S_new = jnp.exp(b_last)[..., None] * S + jnp.einsum('bshk,bshv->bhkv', k_dec, vc)
        return S_new, o_intra + o_inter

    S_fin, o = lax.scan(step, s0.astype(jnp.float32), (blocks(q), blocks(k), blocks(v), blocks(log_a)))
    o = jnp.moveaxis(o, 0, 1).reshape(B, L, H, V)
    return o, S_fin


def mixer_layer(x, conv_buf, gla_state, norm_g, w_in, conv_w, w_alpha2, b_alpha,
                gla_norm_g, w_branch_a, w_branch_b, w_out):
    B, L, _ = x.shape
    h = rmsnorm(x, norm_g)
    proj = jnp.einsum('bld,dp->blp', h, w_in)
    idx = np.cumsum(SPLITS)[:-1].tolist()
    ga, gb, cB, cC, cx, cg, q, k, v, gg, lr = jnp.split(proj, idx, axis=-1)

    u = cC * cx
    u_pad = jnp.concatenate([conv_buf.astype(u.dtype), u], axis=1)
    conv = u_pad[:, 0:L] * conv_w[0]
    for j in range(1, CONV_WIDTH):
        conv = conv + u_pad[:, j:j + L] * conv_w[j]
    za = cB * conv * jax.nn.silu(cg)
    branch_a = jnp.einsum('blc,cd->bld', za, w_branch_a)
    new_buf = u_pad[:, L:]

    qh = q.reshape(B, L, GLA_HEADS, GLA_DK) * (GLA_DK ** -0.5)
    kh = k.reshape(B, L, GLA_HEADS, GLA_DK)
    vh = v.reshape(B, L, GLA_HEADS, GLA_DV)
    a_logit = (jnp.einsum('blr,rk->blk', lr, w_alpha2) + b_alpha).astype(jnp.float32)
    log_a = (jax.nn.log_sigmoid(a_logit) / GLA_TAU).reshape(B, L, GLA_HEADS, GLA_DK)
    o, S_new = gla_chunked(qh, kh, vh, log_a, gla_state)
    o = rmsnorm(o, gla_norm_g).astype(x.dtype)
    zb = o.reshape(B, L, GLA_DV_TOT) * jax.nn.silu(gg)
    branch_b = jnp.einsum('blv,vd->bld', zb, w_branch_b)

    merged = jax.nn.sigmoid(ga) * branch_a + jax.nn.sigmoid(gb) * branch_b
    x = x + jnp.einsum('bld,de->ble', merged, w_out)
    return x, new_buf, S_new


def setup_inputs(seed: int = 0) -> dict:
    key = jax.random.key(seed)
    ks = jax.random.split(key, 16)
    f32 = jnp.float32
    x_prompt = jax.random.normal(ks[0], (BATCH, SEQ, D_MODEL), f32)
    x_sample = jax.random.normal(ks[1], (DEC_BATCH, DEC_SEQ, D_MODEL), f32)
    state_conv = jax.random.normal(ks[2], (DEPTH, DEC_BATCH, CONV_WIDTH - 1, CONV_DIM), f32)
    state_gla = 0.5 * jax.random.normal(ks[3], (DEPTH, DEC_BATCH, GLA_HEADS, GLA_DK, GLA_DV), f32)
    norm_g = 1.0 + 0.02 * jax.random.normal(ks[4], (DEPTH, D_MODEL), f32)
    w_in = jax.random.normal(ks[5], (DEPTH, D_MODEL, IN_WIDTH), f32) * D_MODEL ** -0.5
    conv_w = jax.random.normal(ks[6], (DEPTH, CONV_WIDTH, CONV_DIM), f32) * CONV_WIDTH ** -0.5
    w_alpha2 = jax.random.normal(ks[7], (DEPTH, GLA_RANK, GLA_DK_TOT), f32) * GLA_RANK ** -0.5
    b_alpha = 0.01 * jax.random.normal(ks[8], (DEPTH, GLA_DK_TOT), f32)
    gla_norm_g = 1.0 + 0.02 * jax.random.normal(ks[9], (DEPTH, GLA_DV), f32)
    w_branch_a = jax.random.normal(ks[10], (DEPTH, CONV_DIM, D_MODEL), f32) * CONV_DIM ** -0.5
    w_branch_b = jax.random.normal(ks[11], (DEPTH, GLA_DV_TOT, D_MODEL), f32) * GLA_DV_TOT ** -0.5
    w_out = jax.random.normal(ks[12], (DEPTH, D_MODEL, D_MODEL), f32) * D_MODEL ** -0.5
    final_norm_g = 1.0 + 0.02 * jax.random.normal(ks[13], (D_MODEL,), f32)
    return {"x_prompt": x_prompt, "x_sample": x_sample, "state_conv": state_conv,
            "state_gla": state_gla, "norm_g": norm_g, "w_in": w_in, "conv_w": conv_w,
            "w_alpha2": w_alpha2, "b_alpha": b_alpha, "gla_norm_g": gla_norm_g,
            "w_branch_a": w_branch_a, "w_branch_b": w_branch_b, "w_out": w_out,
            "final_norm_g": final_norm_g}


def reference(x_prompt, x_sample, state_conv, state_gla, norm_g, w_in, conv_w, w_alpha2,
              b_alpha, gla_norm_g, w_branch_a, w_branch_b, w_out, final_norm_g):
    xp = x_prompt
    xs = x_sample
    Bp = x_prompt.shape[0]
    conv_p, gla_p, conv_s, gla_s = [], [], [], []
    for l in range(DEPTH):
        params = (norm_g[l], w_in[l], conv_w[l], w_alpha2[l], b_alpha[l], gla_norm_g[l],
                  w_branch_a[l], w_branch_b[l], w_out[l])
        buf0 = jnp.zeros((Bp, CONV_WIDTH - 1, CONV_DIM), xp.dtype)
        s0 = jnp.zeros((Bp, GLA_HEADS, GLA_DK, GLA_DV), jnp.float32)
        xp, bp, sp = mixer_layer(xp, buf0, s0, *params)
        xs, bs, ss = mixer_layer(xs, state_conv[l], state_gla[l], *params)
        conv_p.append(bp)
        gla_p.append(sp)
        conv_s.append(bs)
        gla_s.append(ss)
    y_prompt = rmsnorm(xp, final_norm_g)
    y_sample = rmsnorm(xs, final_norm_g)
    return (y_prompt, y_sample, jnp.stack(conv_p), jnp.stack(gla_p), jnp.stack(conv_s), jnp.stack(gla_s))
```

```python
import functools

import jax
import jax.numpy as jnp
from jax import lax
from jax.experimental import pallas as pl
from jax.experimental.pallas import tpu as pltpu

F32 = jnp.float32
BF16 = jnp.bfloat16
NORM_EPS = 1e-6
GLA_TAU = 16.0
CONV_WIDTH = 3
SAMPLE_LEN = 8
LANES = 128
VMEM_LIMIT = 56 << 20


def _cparams(*sem):
    return pltpu.CompilerParams(dimension_semantics=sem, vmem_limit_bytes=VMEM_LIMIT)


def _dot(a, b):
    return jnp.dot(a, b, preferred_element_type=F32)


def _dot_nt(a, b):
    return lax.dot_general(a, b, (((1,), (1,)), ((), ())), preferred_element_type=F32)


def _split3(x):
    hi = x.astype(BF16)
    r1 = x - hi.astype(F32)
    mid = r1.astype(BF16)
    lo = (r1 - mid.astype(F32)).astype(BF16)
    return hi, mid, lo


def _rmsnorm_kernel(x_ref, g_ref, o_ref):
    x = x_ref[...]
    ms = jnp.mean(x * x, axis=-1, keepdims=True)
    o_ref[...] = (x * lax.rsqrt(ms + NORM_EPS) * g_ref[...]).astype(o_ref.dtype)


def _rmsnorm(x, g, out_dtype):
    m, d = x.shape
    tm = _tile(m, 512)
    return pl.pallas_call(
        _rmsnorm_kernel,
        grid=(m // tm,),
        in_specs=[pl.BlockSpec((tm, d), lambda i: (i, 0)),
                  pl.BlockSpec((1, d), lambda i: (0, 0))],
        out_specs=pl.BlockSpec((tm, d), lambda i: (i, 0)),
        out_shape=jax.ShapeDtypeStruct((m, d), out_dtype),
        compiler_params=_cparams("parallel"),
        name="rmsnorm",
    )(x, g.reshape(1, d))


def _inproj_kernel(h_ref, w_ref, o_ref, *, sig_end, silu_a, silu_b, q_lo, q_hi, q_scale):
    j = pl.program_id(1)
    acc = _dot(h_ref[...], w_ref[...])
    is_sig = j < sig_end
    is_silu = ((j >= silu_a[0]) & (j < silu_a[1])) | ((j >= silu_b[0]) & (j < silu_b[1]))
    is_q = (j >= q_lo) & (j < q_hi)

    @pl.when(is_sig)
    def _():
        o_ref[...] = jax.nn.sigmoid(acc).astype(o_ref.dtype)

    @pl.when(is_silu)
    def _():
        o_ref[...] = (acc * jax.nn.sigmoid(acc)).astype(o_ref.dtype)

    @pl.when(is_q)
    def _():
        o_ref[...] = (acc * q_scale).astype(o_ref.dtype)

    @pl.when(jnp.logical_not(is_sig | is_silu | is_q))
    def _():
        o_ref[...] = acc.astype(o_ref.dtype)


def _inproj(h, w, offs, dk, tm, tn):
    m, d = h.shape
    n = offs["lr"]
    kern = functools.partial(
        _inproj_kernel,
        sig_end=offs["cB"] // tn,
        silu_a=(offs["cg"] // tn, offs["q"] // tn),
        silu_b=(offs["gg"] // tn, offs["lr"] // tn),
        q_lo=offs["q"] // tn, q_hi=offs["k"] // tn, q_scale=float(dk) ** -0.5)
    return pl.pallas_call(
        kern,
        grid=(m // tm, n // tn),
        in_specs=[pl.BlockSpec((tm, d), lambda i, j: (i, 0)),
                  pl.BlockSpec((d, tn), lambda i, j: (0, j))],
        out_specs=pl.BlockSpec((tm, tn), lambda i, j: (i, j)),
        out_shape=jax.ShapeDtypeStruct((m, n), BF16),
        compiler_params=_cparams("parallel", "arbitrary"),
        name="inproj",
    )(h, w)


def _gate_kernel(h_ref, wlr_ref, w2_ref, ba_ref, o_ref):
    lr = _dot(h_ref[...], wlr_ref[...])
    logit = _dot(lr.astype(BF16), w2_ref[...]) + ba_ref[...]
    log_sig = jnp.minimum(logit, 0.0) - jnp.log1p(jnp.exp(-jnp.abs(logit)))
    o_ref[...] = log_sig / GLA_TAU


def _gate(h, wlr, w2, ba):
    m, d = h.shape
    tm = _tile(m, 512)
    rp, dkt = w2.shape
    return pl.pallas_call(
        _gate_kernel,
        grid=(m // tm,),
        in_specs=[pl.BlockSpec((tm, d), lambda i: (i, 0)),
                  pl.BlockSpec((d, rp), lambda i: (0, 0)),
                  pl.BlockSpec((rp, dkt), lambda i: (0, 0)),
                  pl.BlockSpec((1, dkt), lambda i: (0, 0))],
        out_specs=pl.BlockSpec((tm, dkt), lambda i: (i, 0)),
        out_shape=jax.ShapeDtypeStruct((m, dkt), F32),
        compiler_params=_cparams("parallel"),
        name="gate",
    )(h, wlr, w2, ba.reshape(1, dkt))


def _block_row(x, period, idx):
    c, w = x.shape
    x3 = x.reshape(c // period, period, w)
    return jnp.broadcast_to(x3[:, idx:idx + 1, :], x3.shape).reshape(c, w)


def _level_exponent(b, la, m):
    c, w = b.shape
    pos = lax.broadcasted_iota(jnp.int32, (c, w), 0) & (2 * m - 1)
    upper = pos >= m
    if 2 * m >= 8:
        ref = _block_row(b, 2 * m, m - 1)
        return upper, jnp.where(upper, b - ref, ref - b)
    if m == 1:
        return upper, jnp.where(upper, la, 0.0)
    assert m == 2
    la_prev = pltpu.roll(la, 1, 0)
    la_next = pltpu.roll(la, c - 1, 0)
    e = jnp.where(pos == 0, la_next, jnp.where(pos == 1, 0.0, jnp.where(pos == 2, la, la + la_prev)))
    return upper, e


def _intra_scores(q, k, b, la, levels):
    c = q.shape[0]
    row = lax.broadcasted_iota(jnp.int32, (c, c), 0)
    col = lax.broadcasted_iota(jnp.int32, (c, c), 1)
    x = row ^ col
    lower = col < row
    a = jnp.where(row == col, _dot_nt(q.astype(BF16), k.astype(BF16)), 0.0)
    for m in levels:
        upper, e = _level_exponent(b, la, m)
        g = (jnp.where(upper, q, k) * jnp.exp(e)).astype(BF16)
        s = _dot_nt(g, g)
        a = jnp.where(lower & (x >= m) & (x < 2 * m), s, a)
    return a


def _head_norm_gate(o, gn, gg):
    ms = jnp.mean(o * o, axis=-1, keepdims=True)
    return o * lax.rsqrt(ms + NORM_EPS) * gn * gg


def _levels(c):
    out, m = [], c // 2
    while m >= 1:
        out.append(m)
        m //= 2
    return tuple(out)


def _gla_prompt_kernel(q_ref, k_ref, v_ref, gg_ref, la_ref, gn_ref, zb_ref, s_ref, *, levels):
    n = pl.program_id(2)
    c, dk = q_ref.shape

    @pl.when(n == 0)
    def _():
        s_ref[...] = jnp.zeros_like(s_ref)

    la = la_ref[...]
    row = lax.broadcasted_iota(jnp.int32, (c, c), 0)
    col = lax.broadcasted_iota(jnp.int32, (c, c), 1)
    tril = (col <= row).astype(BF16)
    hi, mid, lo = _split3(la)
    b = _dot(tril, hi) + _dot(tril, mid) + _dot(tril, lo)
    b_last = b[c - 1:c, :]

    q = q_ref[...].astype(F32)
    k = k_ref[...].astype(F32)
    v = v_ref[...]
    a = _intra_scores(q, k, b, la, levels)
    s_old = s_ref[0, 0]
    o = _dot(a.astype(BF16), v) + _dot((q * jnp.exp(b)).astype(BF16), s_old.astype(BF16))
    zb_ref[...] = _head_norm_gate(o, gn_ref[...], gg_ref[...].astype(F32)).astype(zb_ref.dtype)

    kd_t = (k * jnp.exp(b_last - b)).T.astype(BF16)
    eye = lax.broadcasted_iota(jnp.int32, (dk, dk), 0) == lax.broadcasted_iota(jnp.int32, (dk, dk), 1)
    d_col = jnp.sum(jnp.where(eye, jnp.exp(b_last), 0.0), axis=1, keepdims=True)
    s_ref[0, 0] = d_col * s_old + _dot(kd_t, v)


def _gla_prompt(p, la, gn, offs, batch, seq, heads, dk, dv, c):
    nc = seq // c
    kern = functools.partial(_gla_prompt_kernel, levels=_levels(c))
    qb, kb, vb, gb = offs["q"] // dk, offs["k"] // dk, offs["v"] // dv, offs["gg"] // dv
    rows = lambda b, h, n: b * nc + n
    return pl.pallas_call(
        kern,
        grid=(batch, heads, nc),
        in_specs=[pl.BlockSpec((c, dk), lambda b, h, n: (rows(b, h, n), qb + h)),
                  pl.BlockSpec((c, dk), lambda b, h, n: (rows(b, h, n), kb + h)),
                  pl.BlockSpec((c, dv), lambda b, h, n: (rows(b, h, n), vb + h)),
                  pl.BlockSpec((c, dv), lambda b, h, n: (rows(b, h, n), gb + h)),
                  pl.BlockSpec((c, dk), lambda b, h, n: (rows(b, h, n), h)),
                  pl.BlockSpec((1, dv), lambda b, h, n: (0, 0))],
        out_specs=[pl.BlockSpec((c, dv), lambda b, h, n: (rows(b, h, n), h)),
                   pl.BlockSpec((1, 1, dk, dv), lambda b, h, n: (b, h, 0, 0))],
        out_shape=[jax.ShapeDtypeStruct((batch * seq, heads * dv), BF16),
                   jax.ShapeDtypeStruct((batch, heads, dk, dv), F32)],
        compiler_params=_cparams("parallel", "parallel", "arbitrary"),
        name="gla_prompt",
    )(p, p, p, p, la, gn.reshape(1, dv))


def _gla_sample_kernel(q_ref, k_ref, v_ref, gg_ref, la_ref, gn_ref, sin_ref, zb_ref, sout_ref, o_scr):
    r, dk = q_ref.shape
    nseq = r // SAMPLE_LEN
    la = la_ref[...]
    row = lax.broadcasted_iota(jnp.int32, (r, r), 0)
    col = lax.broadcasted_iota(jnp.int32, (r, r), 1)
    tril = ((col <= row) & ((row ^ col) < SAMPLE_LEN)).astype(BF16)
    hi, mid, lo = _split3(la)
    b = _dot(tril, hi) + _dot(tril, mid) + _dot(tril, lo)
    b_last = _block_row(b, SAMPLE_LEN, SAMPLE_LEN - 1)

    q = q_ref[...].astype(F32)
    k = k_ref[...].astype(F32)
    v = v_ref[...]
    a = _intra_scores(q, k, b, la, _levels(SAMPLE_LEN))
    o_scr[...] = _dot(a.astype(BF16), v)
    qe = (q * jnp.exp(b)).astype(BF16)
    kd_t = (k * jnp.exp(b_last - b)).T
    d_t = jnp.exp(b).T
    lane_seq = lax.broadcasted_iota(jnp.int32, (dk, r), 1) // SAMPLE_LEN
    for s in range(nseq):
        lo_r, hi_r = s * SAMPLE_LEN, (s + 1) * SAMPLE_LEN
        s_old = sin_ref[s]
        o_scr[lo_r:hi_r, :] += _dot(qe[lo_r:hi_r, :], s_old.astype(BF16))
        kd_s = jnp.where(lane_seq == s, kd_t, 0.0).astype(BF16)
        sout_ref[s] = d_t[:, hi_r - 1:hi_r] * s_old + _dot(kd_s, v)
    zb_ref[...] = _head_norm_gate(o_scr[...], gn_ref[...], gg_ref[...].astype(F32)).astype(zb_ref.dtype)


def _gla_sample_kernel_aliased(q_ref, k_ref, v_ref, gg_ref, la_ref, gn_ref, sin_ref, prev_ref, zb_ref, sout_ref, o_scr):
    del prev_ref
    _gla_sample_kernel(q_ref, k_ref, v_ref, gg_ref, la_ref, gn_ref, sin_ref, zb_ref, sout_ref, o_scr)


def _gla_sample(p, la, gn, state, new_state, layer, offs, row0, sb):
    _, nseq, heads, dk, dv = state.shape
    r = sb * SAMPLE_LEN
    rb0 = row0 // r
    qb, kb, vb, gb = offs["q"] // dk, offs["k"] // dk, offs["v"] // dv, offs["gg"] // dv
    st_spec = pl.BlockSpec((None, sb, None, dk, dv), lambda i, h: (layer, i, h, 0, 0))
    in_specs = [pl.BlockSpec((r, dk), lambda i, h: (rb0 + i, qb + h)),
                pl.BlockSpec((r, dk), lambda i, h: (rb0 + i, kb + h)),
                pl.BlockSpec((r, dv), lambda i, h: (rb0 + i, vb + h)),
                pl.BlockSpec((r, dv), lambda i, h: (rb0 + i, gb + h)),
                pl.BlockSpec((r, dk), lambda i, h: (rb0 + i, h)),
                pl.BlockSpec((1, dv), lambda i, h: (0, 0)),
                st_spec]
    args = [p, p, p, p, la, gn.reshape(1, dv), state]
    kern, aliases = _gla_sample_kernel, {}
    if new_state is not None:
        in_specs.append(pl.BlockSpec(memory_space=pl.ANY))
        args.append(new_state)
        aliases = {len(args) - 1: 1}
        kern = _gla_sample_kernel_aliased
    return pl.pallas_call(
        kern,
        grid=(nseq // sb, heads),
        in_specs=in_specs,
        out_specs=[pl.BlockSpec((r, dv), lambda i, h: (i, h)), st_spec],
        out_shape=[jax.ShapeDtypeStruct((nseq * SAMPLE_LEN, heads * dv), BF16),
                   jax.ShapeDtypeStruct(state.shape, F32)],
        scratch_shapes=[pltpu.VMEM((r, dv), F32)],
        input_output_aliases=aliases,
        compiler_params=_cparams("parallel", "parallel"),
        name="gla_sample",
    )(*args)


def _conv_prompt_kernel(cb_ref, cc_ref, cx_ref, cg_ref, pc_ref, px_ref, w_ref, za_ref, nb_ref, *, tiles_per_seq):
    i = pl.program_id(0)
    tm = cc_ref.shape[0]
    u = cc_ref[...].astype(F32) * cx_ref[...].astype(F32)
    halo = pc_ref[...].astype(F32) * px_ref[...].astype(F32)
    halo = jnp.where(i % tiles_per_seq == 0, 0.0, halo)
    hr = halo.shape[0]
    row = lax.broadcasted_iota(jnp.int32, u.shape, 0)
    um1 = jnp.where(row == 0, halo[hr - 1:hr, :], pltpu.roll(u, 1, 0))
    um2 = jnp.where(row == 0, halo[hr - 2:hr - 1, :],
                    jnp.where(row == 1, halo[hr - 1:hr, :], pltpu.roll(u, 2, 0)))
    w = w_ref[...]
    conv = um2 * w[0:1, :] + um1 * w[1:2, :] + u * w[2:3, :]
    za_ref[...] = (cb_ref[...].astype(F32) * conv * cg_ref[...].astype(F32)).astype(za_ref.dtype)
    nb_ref[0] = u[tm - 2:tm, :]


def _conv_prompt(p, conv_w, offs, batch, seq, cd, tm, tn):
    tiles_per_seq = seq // tm
    hr = 16
    kern = functools.partial(_conv_prompt_kernel, tiles_per_seq=tiles_per_seq)
    cbo, cco, cxo, cgo = (offs[n] // tn for n in ("cB", "cC", "cx", "cg"))
    prev = lambda i: jnp.maximum(i * (tm // hr) - 1, 0)
    return pl.pallas_call(
        kern,
        grid=(batch * tiles_per_seq, cd // tn),
        in_specs=[pl.BlockSpec((tm, tn), lambda i, j: (i, cbo + j)),
                  pl.BlockSpec((tm, tn), lambda i, j: (i, cco + j)),
                  pl.BlockSpec((tm, tn), lambda i, j: (i, cxo + j)),
                  pl.BlockSpec((tm, tn), lambda i, j: (i, cgo + j)),
                  pl.BlockSpec((hr, tn), lambda i, j: (prev(i), cco + j)),
                  pl.BlockSpec((hr, tn), lambda i, j: (prev(i), cxo + j)),
                  pl.BlockSpec((CONV_WIDTH, tn), lambda i, j: (0, j))],
        out_specs=[pl.BlockSpec((tm, tn), lambda i, j: (i, j)),
                   pl.BlockSpec((1, CONV_WIDTH - 1, tn), lambda i, j: (i // tiles_per_seq, 0, j))],
        out_shape=[jax.ShapeDtypeStruct((batch * seq, cd), BF16),
                   jax.ShapeDtypeStruct((batch, CONV_WIDTH - 1, cd), F32)],
        compiler_params=_cparams("arbitrary", "arbitrary"),
        name="conv_prompt",
    )(p, p, p, p, p, p, conv_w)


def _conv_sample_kernel(cb_ref, cc_ref, cx_ref, cg_ref, st_ref, w_ref, za_ref, nb_ref):
    r, tn = cc_ref.shape
    nseq = r // SAMPLE_LEN
    u = cc_ref[...].astype(F32) * cx_ref[...].astype(F32)
    st = st_ref[...]
    rows = lambda x: jnp.broadcast_to(x, (nseq, SAMPLE_LEN, tn)).reshape(r, tn)
    s0, s1 = rows(st[:, 0:1, :]), rows(st[:, 1:2, :])
    pos = lax.broadcasted_iota(jnp.int32, u.shape, 0) & (SAMPLE_LEN - 1)
    um1 = jnp.where(pos == 0, s1, pltpu.roll(u, 1, 0))
    um2 = jnp.where(pos == 0, s0, jnp.where(pos == 1, s1, pltpu.roll(u, 2, 0)))
    w = w_ref[...]
    conv = um2 * w[0:1, :] + um1 * w[1:2, :] + u * w[2:3, :]
    za_ref[...] = (cb_ref[...].astype(F32) * conv * cg_ref[...].astype(F32)).astype(za_ref.dtype)
    nb_ref[...] = u.reshape(nseq, SAMPLE_LEN, tn)[:, SAMPLE_LEN - 2:SAMPLE_LEN, :]


def _conv_sample(p, conv_w, state, layer, offs, row0, cd, sb, tn):
    nseq = state.shape[1]
    r = sb * SAMPLE_LEN
    rb0 = row0 // r
    cbo, cco, cxo, cgo = (offs[n] // tn for n in ("cB", "cC", "cx", "cg"))
    return pl.pallas_call(
        _conv_sample_kernel,
        grid=(nseq // sb, cd // tn),
        in_specs=[pl.BlockSpec((r, tn), lambda i, j: (rb0 + i, cbo + j)),
                  pl.BlockSpec((r, tn), lambda i, j: (rb0 + i, cco + j)),
                  pl.BlockSpec((r, tn), lambda i, j: (rb0 + i, cxo + j)),
                  pl.BlockSpec((r, tn), lambda i, j: (rb0 + i, cgo + j)),
                  pl.BlockSpec((None, sb, CONV_WIDTH - 1, tn), lambda i, j: (layer, i, 0, j)),
                  pl.BlockSpec((CONV_WIDTH, tn), lambda i, j: (0, j))],
        out_specs=[pl.BlockSpec((r, tn), lambda i, j: (i, j)),
                   pl.BlockSpec((sb, CONV_WIDTH - 1, tn), lambda i, j: (i, 0, j))],
        out_shape=[jax.ShapeDtypeStruct((nseq * SAMPLE_LEN, cd), BF16),
                   jax.ShapeDtypeStruct(state.shape[1:], F32)],
        compiler_params=_cparams("parallel", "parallel"),
        name="conv_sample",
    )(p, p, p, p, state, conv_w)


def _branch_kernel(zap_ref, zbp_ref, zas_ref, zbs_ref, wa_ref, wb_ref, sga_ref, sgb_ref, o_ref, *, prompt_tiles):
    i = pl.program_id(0)

    def merge(za, zb):
        a = _dot(za, wa_ref[...])
        b = _dot(zb, wb_ref[...])
        o_ref[...] = (sga_ref[...].astype(F32) * a + sgb_ref[...].astype(F32) * b).astype(o_ref.dtype)

    @pl.when(i < prompt_tiles)
    def _():
        merge(zap_ref[...], zbp_ref[...])

    @pl.when(i >= prompt_tiles)
    def _():
        merge(zas_ref[...], zbs_ref[...])


def _branch(zap, zbp, zas, zbs, wa, wb, p, offs, tm, tn):
    mp, cd = zap.shape
    ms, dvt = zbs.shape
    d = wa.shape[1]
    pt = mp // tm
    kern = functools.partial(_branch_kernel, prompt_tiles=pt)
    pidx = lambda i: jnp.minimum(i, pt - 1)
    sidx = lambda i: jnp.maximum(i - pt, 0)
    gao, gbo = offs["ga"] // tn, offs["gb"] // tn
    return pl.pallas_call(
        kern,
        grid=((mp + ms) // tm, d // tn),
        in_specs=[pl.BlockSpec((tm, cd), lambda i, j: (pidx(i), 0)),
                  pl.BlockSpec((tm, dvt), lambda i, j: (pidx(i), 0)),
                  pl.BlockSpec((tm, cd), lambda i, j: (sidx(i), 0)),
                  pl.BlockSpec((tm, dvt), lambda i, j: (sidx(i), 0)),
                  pl.BlockSpec((cd, tn), lambda i, j: (0, j)),
                  pl.BlockSpec((dvt, tn), lambda i, j: (0, j)),
                  pl.BlockSpec((tm, tn), lambda i, j: (i, gao + j)),
                  pl.BlockSpec((tm, tn), lambda i, j: (i, gbo + j))],
        out_specs=pl.BlockSpec((tm, tn), lambda i, j: (i, j)),
        out_shape=jax.ShapeDtypeStruct((mp + ms, d), BF16),
        compiler_params=_cparams("parallel", "arbitrary"),
        name="branch",
    )(zap, zbp, zas, zbs, wa, wb, p, p)


def _out_kernel(m_ref, w_ref, x_ref, g_ref, xo_ref, ho_ref):
    x = x_ref[...] + _dot(m_ref[...], w_ref[...])
    xo_ref[...] = x
    ms = jnp.mean(x * x, axis=-1, keepdims=True)
    ho_ref[...] = (x * lax.rsqrt(ms + NORM_EPS) * g_ref[...]).astype(ho_ref.dtype)


def _out(merged, wo, x, g, h_dtype, tm):
    m, d = x.shape
    return pl.pallas_call(
        _out_kernel,
        grid=(m // tm,),
        in_specs=[pl.BlockSpec((tm, d), lambda i: (i, 0)),
                  pl.BlockSpec((d, d), lambda i: (0, 0)),
                  pl.BlockSpec((tm, d), lambda i: (i, 0)),
                  pl.BlockSpec((1, d), lambda i: (0, 0))],
        out_specs=[pl.BlockSpec((tm, d), lambda i: (i, 0)),
                   pl.BlockSpec((tm, d), lambda i: (i, 0))],
        out_shape=[jax.ShapeDtypeStruct((m, d), F32),
                   jax.ShapeDtypeStruct((m, d), h_dtype)],
        compiler_params=_cparams("parallel"),
        name="outproj",
    )(merged, wo, x, g.reshape(1, d))


def _tile(n, pref):
    t = min(n, pref)
    while n % t:
        t //= 2
    return t


def kernel(x_prompt, x_sample, state_conv, state_gla, norm_g, w_in, conv_w, w_alpha2, b_alpha,
           gla_norm_g, w_branch_a, w_branch_b, w_out, final_norm_g):
    batch, seq, d = x_prompt.shape
    nsamp, slen, _ = x_sample.shape
    depth, _, _, cd = state_conv.shape
    _, _, heads, dk, dv = state_gla.shape
    rank, dkt = w_alpha2.shape[1:]
    dvt = heads * dv
    assert slen == SAMPLE_LEN and dkt == heads * dk and state_conv.shape[2] == CONV_WIDTH - 1
    widths = (("ga", d), ("gb", d), ("cB", cd), ("cC", cd), ("cx", cd), ("cg", cd),
              ("q", dkt), ("k", dkt), ("v", dvt), ("gg", dvt), ("lr", rank))
    offs, o = {}, 0
    for name, wd in widths:
        offs[name] = o
        o += wd
    assert o == w_in.shape[2]

    mp, ms = batch * seq, nsamp * slen
    tm = _tile(ms, 1024)
    tn = _tile(dkt, 1024)
    chunk = _tile(seq, 256)
    sb = _tile(nsamp, 16)
    rank_pad = max(rank, LANES)

    x = jnp.concatenate([x_prompt.reshape(mp, d), x_sample.reshape(ms, d)], axis=0)
    h = _rmsnorm(x, norm_g[0], BF16)
    conv_p, gla_p, conv_s, gla_s = [], [], [], None
    for l in range(depth):
        w_main = w_in[l, :, :offs["lr"]].astype(BF16)
        w_lr = jnp.pad(w_in[l, :, offs["lr"]:], ((0, 0), (0, rank_pad - rank))).astype(BF16)
        w2 = jnp.pad(w_alpha2[l], ((0, rank_pad - rank), (0, 0))).astype(BF16)
        p = _inproj(h, w_main, offs, dk, tm, tn)
        la = _gate(h, w_lr, w2, b_alpha[l])
        zb_p, s_p = _gla_prompt(p, la, gla_norm_g[l], offs, batch, seq, heads, dk, dv, chunk)
        zb_s, gla_s = _gla_sample(p, la, gla_norm_g[l], state_gla, gla_s, l, offs, mp, sb)
        za_p, nb_p = _conv_prompt(p, conv_w[l], offs, batch, seq, cd, _tile(seq, 256), _tile(cd, 1024))
        za_s, nb_s = _conv_sample(p, conv_w[l], state_conv, l, offs, mp, cd, sb, _tile(cd, 1024))
        merged = _branch(za_p, zb_p, za_s, zb_s, w_branch_a[l].astype(BF16), w_branch_b[l].astype(BF16),
                         p, offs, tm, _tile(d, 512))
        last = l == depth - 1
        g_next = final_norm_g if last else norm_g[l + 1]
        x, h = _out(merged, w_out[l].astype(BF16), x, g_next, F32 if last else BF16, _tile(ms, 256))
        conv_p.append(nb_p)
        gla_p.append(s_p)
        conv_s.append(nb_s)
    y_prompt = h[:mp].reshape(batch, seq, d)
    y_sample = h[mp:].reshape(nsamp, slen, d)
    return (y_prompt, y_sample, jnp.stack(conv_p), jnp.stack(gla_p), jnp.stack(conv_s), gla_s)
```

```python
import functools

import jax
import jax.numpy as jnp
from jax import lax
from jax.experimental import pallas as pl
from jax.experimental.pallas import tpu as pltpu

F32 = jnp.float32
BF16 = jnp.bfloat16
NORM_EPS = 1e-6
GLA_TAU = 16.0
CONV_WIDTH = 3
SAMPLE_LEN = 8
LANES = 128
MXU_WIDTH = 256
VMEM_LIMIT = 56 << 20


def _cparams(*sem):
    return pltpu.CompilerParams(dimension_semantics=sem, vmem_limit_bytes=VMEM_LIMIT)


def _dot(a, b):
    return jnp.dot(a, b, preferred_element_type=F32)


def _dot_nt(a, b):
    return lax.dot_general(a, b, (((1,), (1,)), ((), ())), preferred_element_type=F32)


def _split3(x):
    hi = x.astype(BF16)
    r1 = x - hi.astype(F32)
    mid = r1.astype(BF16)
    lo = (r1 - mid.astype(F32)).astype(BF16)
    return hi, mid, lo


def _tile(n, pref):
    t = min(n, pref)
    while n % t:
        t //= 2
    return t


def _sigmoid(x):
    return 0.5 * jnp.tanh(0.5 * x) + 0.5


def _rms(x, g):
    ms = jnp.mean(x * x, axis=-1, keepdims=True)
    return x * lax.rsqrt(ms + NORM_EPS) * g


def _embed_kernel(xp_ref, xs_ref, g_ref, x_ref, h_ref, *, prompt_tiles):
    i = pl.program_id(0)

    def emit(x):
        x_ref[...] = x
        h_ref[...] = _rms(x, g_ref[...]).astype(h_ref.dtype)

    @pl.when(i < prompt_tiles)
    def _():
        emit(xp_ref[...])

    @pl.when(i >= prompt_tiles)
    def _():
        emit(xs_ref[...])


def _embed(xp, xs, g):
    (mp, d), ms = xp.shape, xs.shape[0]
    tm = _tile(ms, 512)
    pt = mp // tm
    kern = functools.partial(_embed_kernel, prompt_tiles=pt)
    return pl.pallas_call(
        kern,
        grid=((mp + ms) // tm,),
        in_specs=[pl.BlockSpec((tm, d), lambda i: (jnp.minimum(i, pt - 1), 0)),
                  pl.BlockSpec((tm, d), lambda i: (jnp.maximum(i - pt, 0), 0)),
                  pl.BlockSpec((1, d), lambda i: (0, 0))],
        out_specs=[pl.BlockSpec((tm, d), lambda i: (i, 0)),
                   pl.BlockSpec((tm, d), lambda i: (i, 0))],
        out_shape=[jax.ShapeDtypeStruct((mp + ms, d), F32),
                   jax.ShapeDtypeStruct((mp + ms, d), BF16)],
        compiler_params=_cparams("arbitrary"),
        name="embed",
    )(xp, xs, g.reshape(1, d))


def _inproj_kernel(h_ref, w_ref, o_ref, wb_ref, *, sig_end, silu_a, silu_b, q_lo, q_hi, q_scale):
    j = pl.program_id(0)

    @pl.when(pl.program_id(1) == 0)
    def _():
        wb_ref[...] = w_ref[...].astype(BF16)

    is_sig = j < sig_end
    is_silu = ((j >= silu_a[0]) & (j < silu_a[1])) | ((j >= silu_b[0]) & (j < silu_b[1]))
    is_q = (j >= q_lo) & (j < q_hi)
    acc = _dot(h_ref[...], wb_ref[...])

    @pl.when(is_sig)
    def _():
        o_ref[...] = _sigmoid(acc).astype(o_ref.dtype)

    @pl.when(is_silu)
    def _():
        o_ref[...] = (acc * _sigmoid(acc)).astype(o_ref.dtype)

    @pl.when(is_q)
    def _():
        o_ref[...] = (acc * q_scale).astype(o_ref.dtype)

    @pl.when(jnp.logical_not(is_sig | is_silu | is_q))
    def _():
        o_ref[...] = acc.astype(o_ref.dtype)


def _inproj(h, w_in, layer, offs, dk, tm, tn):
    m, d = h.shape
    n = offs["lr"]
    kern = functools.partial(
        _inproj_kernel,
        sig_end=offs["cB"] // tn,
        silu_a=(offs["cg"] // tn, offs["q"] // tn),
        silu_b=(offs["gg"] // tn, offs["lr"] // tn),
        q_lo=offs["q"] // tn, q_hi=offs["k"] // tn, q_scale=float(dk) ** -0.5)
    return pl.pallas_call(
        kern,
        grid=(n // tn, m // tm),
        in_specs=[pl.BlockSpec((tm, d), lambda j, i: (i, 0)),
                  pl.BlockSpec((None, d, tn), lambda j, i: (layer, 0, j))],
        out_specs=pl.BlockSpec((tm, tn), lambda j, i: (i, j)),
        out_shape=jax.ShapeDtypeStruct((m, n), BF16),
        scratch_shapes=[pltpu.VMEM((d, tn), BF16)],
        compiler_params=_cparams("arbitrary", "arbitrary"),
        name="inproj",
    )(h, w_in)


def _gate_kernel(h_ref, wlr_ref, w2_ref, ba_ref, o_ref, *, rank):
    lr = _dot(h_ref[...], wlr_ref[...].astype(BF16))
    lr = jnp.where(lax.broadcasted_iota(jnp.int32, lr.shape, 1) < rank, lr, 0.0)
    logit = _dot(lr.astype(BF16), w2_ref[...]) + ba_ref[...]
    log_sig = jnp.minimum(logit, 0.0) - jnp.log1p(jnp.exp(-jnp.abs(logit)))
    o_ref[...] = log_sig / GLA_TAU


def _gate(h, w_in, w2, ba, layer, off_lr, rank):
    m, d = h.shape
    rp, dkt = w2.shape[1:]
    tm = _tile(m, 512)
    return pl.pallas_call(
        functools.partial(_gate_kernel, rank=rank),
        grid=(m // tm,),
        in_specs=[pl.BlockSpec((tm, d), lambda i: (i, 0)),
                  pl.BlockSpec((None, d, rp), lambda i: (layer, 0, off_lr // rp)),
                  pl.BlockSpec((None, rp, dkt), lambda i: (layer, 0, 0)),
                  pl.BlockSpec((None, 1, dkt), lambda i: (layer, 0, 0))],
        out_specs=pl.BlockSpec((tm, dkt), lambda i: (i, 0)),
        out_shape=jax.ShapeDtypeStruct((m, dkt), F32),
        compiler_params=_cparams("parallel"),
        name="gate",
    )(h, w_in, w2, ba.reshape(ba.shape[0], 1, dkt))


def _block_row(x, period, idx):
    c, w = x.shape
    x3 = x.reshape(c // period, period, w)
    return jnp.broadcast_to(x3[:, idx:idx + 1, :], x3.shape).reshape(c, w)


def _level_exponent(b, la, m):
    c, w = b.shape
    pos = lax.broadcasted_iota(jnp.int32, (c, w), 0) & (2 * m - 1)
    upper = pos >= m
    if 2 * m >= 8:
        ref = _block_row(b, 2 * m, m - 1)
        return upper, jnp.where(upper, b - ref, ref - b)
    if m == 1:
        return upper, jnp.where(upper, la, 0.0)
    assert m == 2
    la_prev = pltpu.roll(la, 1, 0)
    la_next = pltpu.roll(la, c - 1, 0)
    e = jnp.where(pos == 0, la_next, jnp.where(pos == 1, 0.0, jnp.where(pos == 2, la, la + la_prev)))
    return upper, e


def _intra_scores(q, k, b, la, levels):
    c = q.shape[0]
    row = lax.broadcasted_iota(jnp.int32, (c, c), 0)
    col = lax.broadcasted_iota(jnp.int32, (c, c), 1)
    x = row ^ col
    lower = col < row
    a = jnp.where(row == col, _dot_nt(q.astype(BF16), k.astype(BF16)), 0.0)
    for m in levels:
        upper, e = _level_exponent(b, la, m)
        g = (jnp.where(upper, q, k) * jnp.exp(e)).astype(BF16)
        s = _dot_nt(g, g)
        a = jnp.where(lower & (x >= m) & (x < 2 * m), s, a)
    return a


def _head_norm_gate(o, gn, gg):
    return _rms(o, gn) * gg


def _levels(c):
    out, m = [], c // 2
    while m >= 1:
        out.append(m)
        m //= 2
    return tuple(out)


def _gla_prompt_kernel(q_ref, k_ref, v_ref, gg_ref, la_ref, gn_ref, zb_ref, s_ref, *, chunk):
    dk = q_ref.shape[1]
    c = chunk

    @pl.when(pl.program_id(2) == 0)
    def _():
        s_ref[...] = jnp.zeros_like(s_ref)

    def step(ci, carry):
        rows = pl.ds(pl.multiple_of(ci * c, c), c)
        la = la_ref[rows, :]
        row = lax.broadcasted_iota(jnp.int32, (c, c), 0)
        col = lax.broadcasted_iota(jnp.int32, (c, c), 1)
        tril = (col <= row).astype(BF16)
        hi, mid, lo = _split3(la)
        b = _dot(tril, hi) + _dot(tril, mid) + _dot(tril, lo)
        b_last = b[c - 1:c, :]

        q = q_ref[rows, :].astype(F32)
        k = k_ref[rows, :].astype(F32)
        v = v_ref[rows, :]
        a = _intra_scores(q, k, b, la, _levels(c))
        s_old = s_ref[...]
        o = _dot(a.astype(BF16), v) + _dot((q * jnp.exp(b)).astype(BF16), s_old.astype(BF16))
        zb_ref[rows, :] = _head_norm_gate(o, gn_ref[...], gg_ref[rows, :].astype(F32)).astype(zb_ref.dtype)

        kd_t = (k * jnp.exp(b_last - b)).T.astype(BF16)
        eye = lax.broadcasted_iota(jnp.int32, (dk, dk), 0) == lax.broadcasted_iota(jnp.int32, (dk, dk), 1)
        d_col = jnp.sum(jnp.where(eye, jnp.exp(b_last), 0.0), axis=1, keepdims=True)
        s_ref[...] = d_col * s_old + _dot(kd_t, v)
        return carry

    lax.fori_loop(0, q_ref.shape[0] // c, step, 0)


def _gla_prompt(p, la, gn, layer, offs, batch, seq, rows_total, heads, dk, dv, chunk, tblk):
    nb = seq // tblk
    kern = functools.partial(_gla_prompt_kernel, chunk=chunk)
    qb, kb, vb, gb = offs["q"] // dk, offs["k"] // dk, offs["v"] // dv, offs["gg"] // dv
    rows = lambda b, h, n: b * nb + n
    return pl.pallas_call(
        kern,
        grid=(batch, heads, nb),
        in_specs=[pl.BlockSpec((tblk, dk), lambda b, h, n: (rows(b, h, n), qb + h)),
                  pl.BlockSpec((tblk, dk), lambda b, h, n: (rows(b, h, n), kb + h)),
                  pl.BlockSpec((tblk, dv), lambda b, h, n: (rows(b, h, n), vb + h)),
                  pl.BlockSpec((tblk, dv), lambda b, h, n: (rows(b, h, n), gb + h)),
                  pl.BlockSpec((tblk, dk), lambda b, h, n: (rows(b, h, n), h)),
                  pl.BlockSpec((None, 1, dv), lambda b, h, n: (layer, 0, 0))],
        out_specs=[pl.BlockSpec((tblk, dv), lambda b, h, n: (rows(b, h, n), h)),
                   pl.BlockSpec((None, None, dk, dv), lambda b, h, n: (b, h, 0, 0))],
        out_shape=[jax.ShapeDtypeStruct((rows_total, heads * dv), BF16),
                   jax.ShapeDtypeStruct((batch, heads, dk, dv), F32)],
        compiler_params=_cparams("parallel", "parallel", "arbitrary"),
        name="gla_prompt",
    )(p, p, p, p, la, gn.reshape(gn.shape[0], 1, dv))


def _gla_sample_body(q_ref, k_ref, v_ref, gg_ref, la_ref, gn_ref, sin_ref, zb_ref, sout_ref, o_scr):
    r, dk = q_ref.shape
    nseq = r // SAMPLE_LEN
    la = la_ref[...]
    row = lax.broadcasted_iota(jnp.int32, (r, r), 0)
    col = lax.broadcasted_iota(jnp.int32, (r, r), 1)
    tril = ((col <= row) & ((row ^ col) < SAMPLE_LEN)).astype(BF16)
    hi, mid, lo = _split3(la)
    b = _dot(tril, hi) + _dot(tril, mid) + _dot(tril, lo)
    b_last = _block_row(b, SAMPLE_LEN, SAMPLE_LEN - 1)

    q = q_ref[...].astype(F32)
    k = k_ref[...].astype(F32)
    v = v_ref[...]
    a = _intra_scores(q, k, b, la, _levels(SAMPLE_LEN))
    o_scr[...] = _dot(a.astype(BF16), v)
    qe = (q * jnp.exp(b)).astype(BF16)
    kd_t = (k * jnp.exp(b_last - b)).T
    d_t = jnp.exp(b).T
    lane_seq = lax.broadcasted_iota(jnp.int32, (dk, r), 1) // SAMPLE_LEN
    for s in range(nseq):
        lo_r, hi_r = s * SAMPLE_LEN, (s + 1) * SAMPLE_LEN
        s_old = sin_ref[s]
        o_scr[lo_r:hi_r, :] += _dot(qe[lo_r:hi_r, :], s_old.astype(BF16))
        kd_s = jnp.where(lane_seq == s, kd_t, 0.0).astype(BF16)
        sout_ref[s] = d_t[:, hi_r - 1:hi_r] * s_old + _dot(kd_s, v)
    zb_ref[...] = _head_norm_gate(o_scr[...], gn_ref[...], gg_ref[...].astype(F32)).astype(zb_ref.dtype)


def _gla_sample_kernel(n_alias, *refs):
    _gla_sample_body(*refs[:7], *refs[7 + n_alias:])


def _gla_sample(p, la, gn, state, zb_all, new_state, layer, offs, row0, sb):
    _, nseq, heads, dk, dv = state.shape
    r = sb * SAMPLE_LEN
    rb0 = row0 // r
    qb, kb, vb, gb = offs["q"] // dk, offs["k"] // dk, offs["v"] // dv, offs["gg"] // dv
    st_spec = pl.BlockSpec((None, sb, None, dk, dv), lambda i, h: (layer, i, h, 0, 0))
    zb_spec = pl.BlockSpec((r, dv), lambda i, h: (rb0 + i, h))
    in_specs = [pl.BlockSpec((r, dk), lambda i, h: (rb0 + i, qb + h)),
                pl.BlockSpec((r, dk), lambda i, h: (rb0 + i, kb + h)),
                pl.BlockSpec((r, dv), lambda i, h: (rb0 + i, vb + h)),
                pl.BlockSpec((r, dv), lambda i, h: (rb0 + i, gb + h)),
                pl.BlockSpec((r, dk), lambda i, h: (rb0 + i, h)),
                pl.BlockSpec((None, 1, dv), lambda i, h: (layer, 0, 0)),
                st_spec,
                pl.BlockSpec(memory_space=pl.ANY)]
    args = [p, p, p, p, la, gn.reshape(gn.shape[0], 1, dv), state, zb_all]
    aliases = {7: 0}
    if new_state is not None:
        in_specs.append(pl.BlockSpec(memory_space=pl.ANY))
        args.append(new_state)
        aliases[8] = 1
    return pl.pallas_call(
        functools.partial(_gla_sample_kernel, len(aliases)),
        grid=(nseq // sb, heads),
        in_specs=in_specs,
        out_specs=[zb_spec, st_spec],
        out_shape=[jax.ShapeDtypeStruct(zb_all.shape, BF16),
                   jax.ShapeDtypeStruct(state.shape, F32)],
        scratch_shapes=[pltpu.VMEM((r, dv), F32)],
        input_output_aliases=aliases,
        compiler_params=_cparams("parallel", "parallel"),
        name="gla_sample",
    )(*args)


def _conv_prompt_kernel(cb_ref, cc_ref, cx_ref, cg_ref, pc_ref, px_ref, w_ref, za_ref, nb_ref, *, tiles_per_seq):
    i = pl.program_id(1)
    tm = cc_ref.shape[0]
    u = cc_ref[...].astype(F32) * cx_ref[...].astype(F32)
    halo = pc_ref[...].astype(F32) * px_ref[...].astype(F32)
    halo = jnp.where(i % tiles_per_seq == 0, 0.0, halo)
    hr = halo.shape[0]
    row = lax.broadcasted_iota(jnp.int32, u.shape, 0)
    um1 = jnp.where(row == 0, halo[hr - 1:hr, :], pltpu.roll(u, 1, 0))
    um2 = jnp.where(row == 0, halo[hr - 2:hr - 1, :],
                    jnp.where(row == 1, halo[hr - 1:hr, :], pltpu.roll(u, 2, 0)))
    w = w_ref[...]
    conv = um2 * w[0:1, :] + um1 * w[1:2, :] + u * w[2:3, :]
    za_ref[...] = (cb_ref[...].astype(F32) * conv * cg_ref[...].astype(F32)).astype(za_ref.dtype)
    nb_ref[...] = u[tm - 2:tm, :]


def _conv_prompt(p, conv_w, layer, offs, batch, seq, rows_total, cd, tm, tn):
    tiles_per_seq = seq // tm
    hr = 16
    kern = functools.partial(_conv_prompt_kernel, tiles_per_seq=tiles_per_seq)
    cbo, cco, cxo, cgo = (offs[n] // tn for n in ("cB", "cC", "cx", "cg"))
    prev = lambda i: jnp.maximum(i * (tm // hr) - 1, 0)
    return pl.pallas_call(
        kern,
        grid=(cd // tn, batch * tiles_per_seq),
        in_specs=[pl.BlockSpec((tm, tn), lambda j, i: (i, cbo + j)),
                  pl.BlockSpec((tm, tn), lambda j, i: (i, cco + j)),
                  pl.BlockSpec((tm, tn), lambda j, i: (i, cxo + j)),
                  pl.BlockSpec((tm, tn), lambda j, i: (i, cgo + j)),
                  pl.BlockSpec((hr, tn), lambda j, i: (prev(i), cco + j)),
                  pl.BlockSpec((hr, tn), lambda j, i: (prev(i), cxo + j)),
                  pl.BlockSpec((None, CONV_WIDTH, tn), lambda j, i: (layer, 0, j))],
        out_specs=[pl.BlockSpec((tm, tn), lambda j, i: (i, j)),
                   pl.BlockSpec((None, CONV_WIDTH - 1, tn), lambda j, i: (i // tiles_per_seq, 0, j))],
        out_shape=[jax.ShapeDtypeStruct((rows_total, cd), BF16),
                   jax.ShapeDtypeStruct((batch, CONV_WIDTH - 1, cd), F32)],
        compiler_params=_cparams("arbitrary", "arbitrary"),
        name="conv_prompt",
    )(p, p, p, p, p, p, conv_w)


def _conv_sample_kernel(cb_ref, cc_ref, cx_ref, cg_ref, st_ref, w_ref, za_all_ref, za_ref, nb_ref):
    del za_all_ref
    r, tn = cc_ref.shape
    nseq = r // SAMPLE_LEN
    u = cc_ref[...].astype(F32) * cx_ref[...].astype(F32)
    st = st_ref[...]
    rows = lambda x: jnp.broadcast_to(x, (nseq, SAMPLE_LEN, tn)).reshape(r, tn)
    s0, s1 = rows(st[:, 0:1, :]), rows(st[:, 1:2, :])
    pos = lax.broadcasted_iota(jnp.int32, u.shape, 0) & (SAMPLE_LEN - 1)
    um1 = jnp.where(pos == 0, s1, pltpu.roll(u, 1, 0))
    um2 = jnp.where(pos == 0, s0, jnp.where(pos == 1, s1, pltpu.roll(u, 2, 0)))
    w = w_ref[...]
    conv = um2 * w[0:1, :] + um1 * w[1:2, :] + u * w[2:3, :]
    za_ref[...] = (cb_ref[...].astype(F32) * conv * cg_ref[...].astype(F32)).astype(za_ref.dtype)
    nb_ref[...] = u.reshape(nseq, SAMPLE_LEN, tn)[:, SAMPLE_LEN - 2:SAMPLE_LEN, :]


def _conv_sample(p, conv_w, state, za_all, layer, offs, row0, cd, sb, tn):
    nseq = state.shape[1]
    r = sb * SAMPLE_LEN
    rb0 = row0 // r
    cbo, cco, cxo, cgo = (offs[n] // tn for n in ("cB", "cC", "cx", "cg"))
    return pl.pallas_call(
        _conv_sample_kernel,
        grid=(nseq // sb, cd // tn),
        in_specs=[pl.BlockSpec((r, tn), lambda i, j: (rb0 + i, cbo + j)),
                  pl.BlockSpec((r, tn), lambda i, j: (rb0 + i, cco + j)),
                  pl.BlockSpec((r, tn), lambda i, j: (rb0 + i, cxo + j)),
                  pl.BlockSpec((r, tn), lambda i, j: (rb0 + i, cgo + j)),
                  pl.BlockSpec((None, sb, CONV_WIDTH - 1, tn), lambda i, j: (layer, i, 0, j)),
                  pl.BlockSpec((None, CONV_WIDTH, tn), lambda i, j: (layer, 0, j)),
                  pl.BlockSpec(memory_space=pl.ANY)],
        out_specs=[pl.BlockSpec((r, tn), lambda i, j: (rb0 + i, j)),
                   pl.BlockSpec((sb, CONV_WIDTH - 1, tn), lambda i, j: (i, 0, j))],
        out_shape=[jax.ShapeDtypeStruct(za_all.shape, BF16),
                   jax.ShapeDtypeStruct(state.shape[1:], F32)],
        input_output_aliases={6: 0},
        compiler_params=_cparams("parallel", "parallel"),
        name="conv_sample",
    )(p, p, p, p, state, conv_w, za_all)


def _branch_kernel(za_ref, zb_ref, wa_ref, wb_ref, sga_ref, sgb_ref, o_ref, wa_bf, wb_bf):
    @pl.when(pl.program_id(1) == 0)
    def _():
        wa_bf[...] = wa_ref[...].astype(BF16)
        wb_bf[...] = wb_ref[...].astype(BF16)

    tn = o_ref.shape[1]
    sub = min(tn, MXU_WIDTH)
    for c in range(tn // sub):
        cols = slice(c * sub, (c + 1) * sub)
        a = _dot(za_ref[...], wa_bf[:, cols])
        b = _dot(zb_ref[...], wb_bf[:, cols])
        o_ref[:, cols] = (sga_ref[:, cols].astype(F32) * a + sgb_ref[:, cols].astype(F32) * b).astype(o_ref.dtype)


def _branch(za, zb, w_a, w_b, p, layer, offs, tm, tn):
    m, cd = za.shape
    dvt = zb.shape[1]
    d = w_a.shape[2]
    gao, gbo = offs["ga"] // tn, offs["gb"] // tn
    return pl.pallas_call(
        _branch_kernel,
        grid=(d // tn, m // tm),
        in_specs=[pl.BlockSpec((tm, cd), lambda j, i: (i, 0)),
                  pl.BlockSpec((tm, dvt), lambda j, i: (i, 0)),
                  pl.BlockSpec((None, cd, tn), lambda j, i: (layer, 0, j)),
                  pl.BlockSpec((None, dvt, tn), lambda j, i: (layer, 0, j)),
                  pl.BlockSpec((tm, tn), lambda j, i: (i, gao + j)),
                  pl.BlockSpec((tm, tn), lambda j, i: (i, gbo + j))],
        out_specs=pl.BlockSpec((tm, tn), lambda j, i: (i, j)),
        out_shape=jax.ShapeDtypeStruct((m, d), BF16),
        scratch_shapes=[pltpu.VMEM((cd, tn), BF16), pltpu.VMEM((dvt, tn), BF16)],
        compiler_params=_cparams("arbitrary", "arbitrary"),
        name="branch",
    )(za, zb, w_a, w_b, p, p)


def _out_kernel(m_ref, w_ref, x_ref, g_ref, xo_ref, ho_ref, w_bf):
    @pl.when(pl.program_id(0) == 0)
    def _():
        w_bf[...] = w_ref[...].astype(BF16)

    x = x_ref[...] + _dot(m_ref[...], w_bf[...])
    xo_ref[...] = x
    ho_ref[...] = _rms(x, g_ref[...]).astype(ho_ref.dtype)


def _out_final_kernel(m_ref, w_ref, x_ref, g_ref, yp_ref, ys_ref, w_bf, *, prompt_tiles):
    i = pl.program_id(0)

    @pl.when(i == 0)
    def _():
        w_bf[...] = w_ref[...].astype(BF16)

    y = _rms(x_ref[...] + _dot(m_ref[...], w_bf[...]), g_ref[...])

    @pl.when(i < prompt_tiles)
    def _():
        yp_ref[...] = y

    @pl.when(i >= prompt_tiles)
    def _():
        ys_ref[...] = y


def _out(merged, w_out, x, g, layer, tm, prompt_rows=None):
    m, d = x.shape
    in_specs = [pl.BlockSpec((tm, d), lambda i: (i, 0)),
                pl.BlockSpec((None, d, d), lambda i: (layer, 0, 0), pipeline_mode=pl.Buffered(1)),
                pl.BlockSpec((tm, d), lambda i: (i, 0)),
                pl.BlockSpec((1, d), lambda i: (0, 0))]
    if prompt_rows is None:
        kern = _out_kernel
        out_specs = [pl.BlockSpec((tm, d), lambda i: (i, 0)), pl.BlockSpec((tm, d), lambda i: (i, 0))]
        out_shape = [jax.ShapeDtypeStruct((m, d), F32), jax.ShapeDtypeStruct((m, d), BF16)]
    else:
        pt = prompt_rows // tm
        kern = functools.partial(_out_final_kernel, prompt_tiles=pt)
        out_specs = [pl.BlockSpec((tm, d), lambda i: (jnp.minimum(i, pt - 1), 0)),
                     pl.BlockSpec((tm, d), lambda i: (jnp.maximum(i - pt, 0), 0))]
        out_shape = [jax.ShapeDtypeStruct((prompt_rows, d), F32), jax.ShapeDtypeStruct((m - prompt_rows, d), F32)]
    return pl.pallas_call(
        kern,
        grid=(m // tm,),
        in_specs=in_specs,
        out_specs=out_specs,
        out_shape=out_shape,
        scratch_shapes=[pltpu.VMEM((d, d), BF16)],
        compiler_params=_cparams("arbitrary"),
        name="outproj",
    )(merged, w_out, x, g.reshape(1, d))


def kernel(x_prompt, x_sample, state_conv, state_gla, norm_g, w_in, conv_w, w_alpha2, b_alpha,
           gla_norm_g, w_branch_a, w_branch_b, w_out, final_norm_g):
    batch, seq, d = x_prompt.shape
    nsamp, slen, _ = x_sample.shape
    depth, _, _, cd = state_conv.shape
    _, _, heads, dk, dv = state_gla.shape
    rank, dkt = w_alpha2.shape[1:]
    dvt = heads * dv
    assert slen == SAMPLE_LEN and dkt == heads * dk and state_conv.shape[2] == CONV_WIDTH - 1
    widths = (("ga", d), ("gb", d), ("cB", cd), ("cC", cd), ("cx", cd), ("cg", cd),
              ("q", dkt), ("k", dkt), ("v", dvt), ("gg", dvt), ("lr", rank))
    offs, o = {}, 0
    for name, wd in widths:
        offs[name] = o
        o += wd
    assert o == w_in.shape[2] and rank <= LANES and offs["lr"] % LANES == 0

    mp, ms = batch * seq, nsamp * slen
    m = mp + ms
    tm = _tile(ms, 1024)
    tn = _tile(dkt, 1024)
    chunk = _tile(seq, 256)
    sb = _tile(nsamp, 16)
    w2 = jnp.pad(w_alpha2, ((0, 0), (0, LANES - rank), (0, 0))).astype(BF16)

    x, h = _embed(x_prompt.reshape(mp, d), x_sample.reshape(ms, d), norm_g[0])
    conv_p, gla_p, conv_s, gla_s = [], [], [], None
    for l in range(depth):
        p = _inproj(h, w_in, l, offs, dk, tm, tn)
        la = _gate(h, w_in, w2, b_alpha, l, offs["lr"], rank)
        zb, s_p = _gla_prompt(p, la, gla_norm_g, l, offs, batch, seq, m, heads, dk, dv, chunk, _tile(seq, 1024))
        zb, gla_s = _gla_sample(p, la, gla_norm_g, state_gla, zb, gla_s, l, offs, mp, sb)
        za, nb_p = _conv_prompt(p, conv_w, l, offs, batch, seq, m, cd, _tile(seq, 256), _tile(cd, 1024))
        za, nb_s = _conv_sample(p, conv_w, state_conv, za, l, offs, mp, cd, sb, _tile(cd, 1024))
        merged = _branch(za, zb, w_branch_a, w_branch_b, p, l, offs, tm, _tile(d, 512))
        conv_p.append(nb_p)
        gla_p.append(s_p)
        conv_s.append(nb_s)
        if l < depth - 1:
            x, h = _out(merged, w_out, x, norm_g[l + 1], l, _tile(ms, 256))
        else:
            y_p, y_s = _out(merged, w_out, x, final_norm_g, l, _tile(ms, 256), prompt_rows=mp)
    return (y_p.reshape(batch, seq, d), y_s.reshape(nsamp, slen, d),
            jnp.stack(conv_p), jnp.stack(gla_p), jnp.stack(conv_s), gla_s)
```

```python
import functools

import jax
import jax.numpy as jnp
from jax import lax
from jax.experimental import pallas as pl
from jax.experimental.pallas import tpu as pltpu

F32 = jnp.float32
BF16 = jnp.bfloat16
NORM_EPS = 1e-6
GLA_TAU = 16.0
LOG2E = 1.4426950408889634
CONV_WIDTH = 3
SAMPLE_LEN = 8
LANES = 128
MXU_WIDTH = 256
VMEM_LIMIT = 56 << 20


def _cparams(*sem):
    return pltpu.CompilerParams(dimension_semantics=sem, vmem_limit_bytes=VMEM_LIMIT)


def _dot(a, b):
    return jnp.dot(a, b, preferred_element_type=F32)


def _dot_nt(a, b):
    return lax.dot_general(a, b, (((1,), (1,)), ((), ())), preferred_element_type=F32)


def _split3(x):
    hi = x.astype(BF16)
    r1 = x - hi.astype(F32)
    mid = r1.astype(BF16)
    lo = (r1 - mid.astype(F32)).astype(BF16)
    return hi, mid, lo


def _tile(n, pref):
    t = min(n, pref)
    while n % t:
        t //= 2
    return t


def _pack_rows(x):
    return pltpu.bitcast(x.astype(BF16), jnp.uint32)


def _unpack_rows(x):
    return pltpu.bitcast(x, BF16)


def _sigmoid(x):
    return 0.5 * jnp.tanh(0.5 * x) + 0.5


def _rms(x, g):
    ms = jnp.mean(x * x, axis=-1, keepdims=True)
    return x * lax.rsqrt(ms + NORM_EPS) * g


def _embed_kernel(xp_ref, xs_ref, g_ref, x_ref, h_ref, *, prompt_tiles):
    i = pl.program_id(0)

    def emit(x):
        x_ref[...] = x
        h_ref[...] = _pack_rows(_rms(x, g_ref[...]))

    @pl.when(i < prompt_tiles)
    def _():
        emit(xp_ref[...])

    @pl.when(i >= prompt_tiles)
    def _():
        emit(xs_ref[...])


def _embed(xp, xs, g):
    (mp, d), ms = xp.shape, xs.shape[0]
    tm = _tile(ms, 512)
    pt = mp // tm
    kern = functools.partial(_embed_kernel, prompt_tiles=pt)
    return pl.pallas_call(
        kern,
        grid=((mp + ms) // tm,),
        in_specs=[pl.BlockSpec((tm, d), lambda i: (jnp.minimum(i, pt - 1), 0)),
                  pl.BlockSpec((tm, d), lambda i: (jnp.maximum(i - pt, 0), 0)),
                  pl.BlockSpec((1, d), lambda i: (0, 0))],
        out_specs=[pl.BlockSpec((tm, d), lambda i: (i, 0)),
                   pl.BlockSpec((tm // 2, d), lambda i: (i, 0))],
        out_shape=[jax.ShapeDtypeStruct((mp + ms, d), F32),
                   jax.ShapeDtypeStruct(((mp + ms) // 2, d), jnp.uint32)],
        compiler_params=_cparams("arbitrary"),
        name="embed",
    )(xp, xs, g.reshape(1, d))


def _inproj_kernel(h_ref, w_ref, o_ref, wb_ref, *, sig_end, silu_a, silu_b, q_lo, q_hi, q_scale):
    j = pl.program_id(0)

    @pl.when(pl.program_id(1) == 0)
    def _():
        wb_ref[...] = w_ref[...].T.astype(BF16)

    is_sig = j < sig_end
    is_silu = ((j >= silu_a[0]) & (j < silu_a[1])) | ((j >= silu_b[0]) & (j < silu_b[1]))
    is_q = (j >= q_lo) & (j < q_hi)
    gated = is_sig | is_silu
    alpha = jnp.where(gated, 0.5, jnp.where(is_q, q_scale, 1.0)).astype(F32)
    beta = jnp.where(gated, 0.5, 0.0).astype(F32)
    tm, tn = o_ref.shape
    sub_n = min(tn, 2 * MXU_WIDTH)
    sub_m = min(tm, 256)
    for c in range(tn // sub_n):
        cols = slice(c * sub_n, (c + 1) * sub_n)
        for r in range(tm // sub_m):
            acc = _dot(_unpack_rows(h_ref[r * sub_m // 2:(r + 1) * sub_m // 2, :]), wb_ref[:, cols])
            gate = alpha + beta * jnp.tanh(0.5 * acc)
            o_ref[r * sub_m:(r + 1) * sub_m, cols] = (jnp.where(is_sig, 1.0, acc) * gate).astype(o_ref.dtype)


def _inproj(h, w_in_t, layer, offs, dk, tm, tn):
    m, d = 2 * h.shape[0], h.shape[1]
    n = offs["lr"]
    kern = functools.partial(
        _inproj_kernel,
        sig_end=offs["cB"] // tn,
        silu_a=(offs["cg"] // tn, offs["q"] // tn),
        silu_b=(offs["gg"] // tn, offs["lr"] // tn),
        q_lo=offs["q"] // tn, q_hi=offs["k"] // tn, q_scale=float(dk) ** -0.5)
    return pl.pallas_call(
        kern,
        grid=(n // tn, m // tm),
        in_specs=[pl.BlockSpec((tm // 2, d), lambda j, i: (i, 0)),
                  pl.BlockSpec((None, tn, d), lambda j, i: (layer, j, 0))],
        out_specs=pl.BlockSpec((tm, tn), lambda j, i: (i, j)),
        out_shape=jax.ShapeDtypeStruct((m, n), BF16),
        scratch_shapes=[pltpu.VMEM((d, tn), BF16)],
        compiler_params=_cparams("arbitrary", "arbitrary"),
        name="inproj",
    )(h, w_in_t)


def _gate_kernel(h_ref, wlr_ref, w2_ref, ba_ref, o_ref, *, rank):
    lr = _dot_nt(_unpack_rows(h_ref[...]), wlr_ref[...].astype(BF16))
    lr = jnp.where(lax.broadcasted_iota(jnp.int32, lr.shape, 1) < rank, lr, 0.0)
    logit = _dot(lr.astype(BF16), w2_ref[...]) + ba_ref[...]
    log_sig = jnp.minimum(logit, 0.0) - jnp.log1p(jnp.exp(-jnp.abs(logit)))
    o_ref[...] = log_sig / GLA_TAU


def _gate(h, w_in_t, w2, ba, layer, off_lr, rank):
    m, d = 2 * h.shape[0], h.shape[1]
    rp, dkt = w2.shape[1:]
    tm = _tile(m, 512)
    return pl.pallas_call(
        functools.partial(_gate_kernel, rank=rank),
        grid=(m // tm,),
        in_specs=[pl.BlockSpec((tm // 2, d), lambda i: (i, 0)),
                  pl.BlockSpec((None, rp, d), lambda i: (layer, off_lr // rp, 0)),
                  pl.BlockSpec((None, rp, dkt), lambda i: (layer, 0, 0)),
                  pl.BlockSpec((None, 1, dkt), lambda i: (layer, 0, 0))],
        out_specs=pl.BlockSpec((tm, dkt), lambda i: (i, 0)),
        out_shape=jax.ShapeDtypeStruct((m, dkt), F32),
        compiler_params=_cparams("parallel"),
        name="gate",
    )(h, w_in_t, w2, ba.reshape(ba.shape[0], 1, dkt))


def _block_row(x, period, idx):
    c, w = x.shape
    x3 = x.reshape(c // period, period, w)
    return jnp.broadcast_to(x3[:, idx:idx + 1, :], x3.shape).reshape(c, w)


def _level_factor(q, k, b2, la2, m):
    c, w = b2.shape
    pos = lax.broadcasted_iota(jnp.int32, (c, w), 0) & (2 * m - 1)
    upper = pos >= m
    if 2 * m >= 8:
        e = -jnp.abs(b2 - _block_row(b2, 2 * m, m - 1))
    elif m == 1:
        e = jnp.where(upper, la2, 0.0)
    else:
        assert m == 2
        la_prev = pltpu.roll(la2, 1, 0)
        la_next = pltpu.roll(la2, c - 1, 0)
        e = jnp.where(pos == 0, la_next, jnp.where(pos == 1, 0.0, jnp.where(pos == 2, la2, la2 + la_prev)))
    return (jnp.where(upper, q, k) * jnp.exp2(e)).astype(BF16)


def _pair_code(n):
    row = lax.broadcasted_iota(jnp.int32, (n, n), 0)
    col = lax.broadcasted_iota(jnp.int32, (n, n), 1)
    xf = (row ^ col).astype(F32)
    high_bit = (lax.bitcast_convert_type(xf, jnp.int32) >> 23) - 127
    return jnp.where(col < row, high_bit, jnp.where(col == row, -1, -2))


def _diag_scores(q, k, b2, la2, levels, code, blocks):
    qb, kb = q.astype(BF16), k.astype(BF16)
    out = [jnp.where(code == -1, _dot_nt(qb[lo:hi], kb[lo:hi]), 0.0) for lo, hi in blocks]
    for m in levels:
        g = _level_factor(q, k, b2, la2, m)
        bit = m.bit_length() - 1
        out = [jnp.where(code == bit, _dot_nt(g[lo:hi], g[lo:hi]), a) for a, (lo, hi) in zip(out, blocks)]
    return out


def _head_norm_gate(o, gn, gg):
    return _rms(o, gn) * gg


def _levels(c):
    out, m = [], c // 2
    while m >= 1:
        out.append(m)
        m //= 2
    return tuple(out)


def _gla_prompt_kernel(q_ref, k_ref, v_ref, gg_ref, la_ref, gn_ref, zb_ref, s_ref, *, chunk):
    dk = q_ref.shape[1]
    c, hc = chunk, chunk // 2

    @pl.when(pl.program_id(2) == 0)
    def _():
        s_ref[...] = jnp.zeros_like(s_ref)

    def step(ci, carry):
        rows = pl.ds(pl.multiple_of(ci * c, c), c)
        la = la_ref[rows, :]
        row = lax.broadcasted_iota(jnp.int32, (c, c), 0)
        col = lax.broadcasted_iota(jnp.int32, (c, c), 1)
        tril = (col <= row).astype(BF16)
        hi, mid, lo = _split3(la)
        b2 = (_dot(tril, hi) + _dot(tril, mid) + _dot(tril, lo)) * LOG2E
        la2 = la * LOG2E
        b2_last = b2[c - 1:c, :]

        q = q_ref[rows, :].astype(F32)
        k = k_ref[rows, :].astype(F32)
        v = v_ref[rows, :]
        levels = _levels(c)
        a00, a11 = _diag_scores(q, k, b2, la2, levels[1:], _pair_code(hc), ((0, hc), (hc, c)))
        g_top = _level_factor(q, k, b2, la2, hc)
        a10 = _dot_nt(g_top[hc:], g_top[:hc])
        s_old = s_ref[...]
        o_intra = jnp.concatenate(
            [_dot(a00.astype(BF16), v[:hc]),
             _dot(jnp.concatenate([a10, a11], axis=1).astype(BF16), v)], axis=0)
        o = o_intra + _dot((q * jnp.exp2(b2)).astype(BF16), s_old.astype(BF16))
        packed_rows = pl.ds(pl.multiple_of(ci * hc, hc), hc)
        zb_ref[packed_rows, :] = _pack_rows(_head_norm_gate(o, gn_ref[...], gg_ref[rows, :].astype(F32)))

        kd_t = (k * jnp.exp2(b2_last - b2)).T.astype(BF16)
        eye = lax.broadcasted_iota(jnp.int32, (dk, dk), 0) == lax.broadcasted_iota(jnp.int32, (dk, dk), 1)
        d_col = jnp.sum(jnp.where(eye, jnp.exp2(b2_last), 0.0), axis=1, keepdims=True)
        s_ref[...] = d_col * s_old + _dot(kd_t, v)
        return carry

    lax.fori_loop(0, q_ref.shape[0] // c, step, 0)


def _gla_prompt(p, la, gn, layer, offs, batch, seq, rows_total, heads, dk, dv, chunk, tblk):
    nb = seq // tblk
    kern = functools.partial(_gla_prompt_kernel, chunk=chunk)
    qb, kb, vb, gb = offs["q"] // dk, offs["k"] // dk, offs["v"] // dv, offs["gg"] // dv
    rows = lambda b, h, n: b * nb + n
    return pl.pallas_call(
        kern,
        grid=(batch, heads, nb),
        in_specs=[pl.BlockSpec((tblk, dk), lambda b, h, n: (rows(b, h, n), qb + h)),
                  pl.BlockSpec((tblk, dk), lambda b, h, n: (rows(b, h, n), kb + h)),
                  pl.BlockSpec((tblk, dv), lambda b, h, n: (rows(b, h, n), vb + h)),
                  pl.BlockSpec((tblk, dv), lambda b, h, n: (rows(b, h, n), gb + h)),
                  pl.BlockSpec((tblk, dk), lambda b, h, n: (rows(b, h, n), h)),
                  pl.BlockSpec((None, 1, dv), lambda b, h, n: (layer, 0, 0))],
        out_specs=[pl.BlockSpec((tblk // 2, dv), lambda b, h, n: (rows(b, h, n), h)),
                   pl.BlockSpec((None, None, dk, dv), lambda b, h, n: (b, h, 0, 0))],
        out_shape=[jax.ShapeDtypeStruct((rows_total // 2, heads * dv), jnp.uint32),
                   jax.ShapeDtypeStruct((batch, heads, dk, dv), F32)],
        compiler_params=_cparams("parallel", "parallel", "arbitrary"),
        name="gla_prompt",
    )(p, p, p, p, la, gn.reshape(gn.shape[0], 1, dv))


def _gla_sample_body(q_ref, k_ref, v_ref, gg_ref, la_ref, gn_ref, sin_ref, zb_ref, sout_ref, o_scr):
    r, dk = q_ref.shape
    nseq = r // SAMPLE_LEN
    la = la_ref[...]
    row = lax.broadcasted_iota(jnp.int32, (r, r), 0)
    col = lax.broadcasted_iota(jnp.int32, (r, r), 1)
    tril = ((col <= row) & ((row ^ col) < SAMPLE_LEN)).astype(BF16)
    hi, mid, lo = _split3(la)
    b2 = (_dot(tril, hi) + _dot(tril, mid) + _dot(tril, lo)) * LOG2E
    b2_last = _block_row(b2, SAMPLE_LEN, SAMPLE_LEN - 1)

    q = q_ref[...].astype(F32)
    k = k_ref[...].astype(F32)
    v = v_ref[...]
    (a,) = _diag_scores(q, k, b2, la * LOG2E, _levels(SAMPLE_LEN), _pair_code(r), ((0, r),))
    o_scr[...] = _dot(a.astype(BF16), v)
    decay = jnp.exp2(b2)
    qe = (q * decay).astype(BF16)
    kd_t = (k * jnp.exp2(b2_last - b2)).T
    d_t = decay.T
    lane_seq = lax.broadcasted_iota(jnp.int32, (dk, r), 1) // SAMPLE_LEN
    for s in range(nseq):
        lo_r, hi_r = s * SAMPLE_LEN, (s + 1) * SAMPLE_LEN
        s_old = sin_ref[s]
        o_scr[lo_r:hi_r, :] += _dot(qe[lo_r:hi_r, :], s_old.astype(BF16))
        kd_s = jnp.where(lane_seq == s, kd_t, 0.0).astype(BF16)
        sout_ref[s] = d_t[:, hi_r - 1:hi_r] * s_old + _dot(kd_s, v)
    zb_ref[...] = _pack_rows(_head_norm_gate(o_scr[...], gn_ref[...], gg_ref[...].astype(F32)))


def _gla_sample_kernel(n_alias, *refs):
    _gla_sample_body(*refs[:7], *refs[7 + n_alias:])


def _gla_sample(p, la, gn, state, zb_all, new_state, layer, offs, row0, sb):
    _, nseq, heads, dk, dv = state.shape
    r = sb * SAMPLE_LEN
    rb0 = row0 // r
    qb, kb, vb, gb = offs["q"] // dk, offs["k"] // dk, offs["v"] // dv, offs["gg"] // dv
    st_spec = pl.BlockSpec((None, sb, None, dk, dv), lambda i, h: (layer, i, h, 0, 0))
    zb_spec = pl.BlockSpec((r // 2, dv), lambda i, h: (rb0 + i, h))
    in_specs = [pl.BlockSpec((r, dk), lambda i, h: (rb0 + i, qb + h)),
                pl.BlockSpec((r, dk), lambda i, h: (rb0 + i, kb + h)),
                pl.BlockSpec((r, dv), lambda i, h: (rb0 + i, vb + h)),
                pl.BlockSpec((r, dv), lambda i, h: (rb0 + i, gb + h)),
                pl.BlockSpec((r, dk), lambda i, h: (rb0 + i, h)),
                pl.BlockSpec((None, 1, dv), lambda i, h: (layer, 0, 0)),
                st_spec,
                pl.BlockSpec(memory_space=pl.ANY)]
    args = [p, p, p, p, la, gn.reshape(gn.shape[0], 1, dv), state, zb_all]
    aliases = {7: 0}
    if new_state is not None:
        in_specs.append(pl.BlockSpec(memory_space=pl.ANY))
        args.append(new_state)
        aliases[8] = 1
    return pl.pallas_call(
        functools.partial(_gla_sample_kernel, len(aliases)),
        grid=(nseq // sb, heads),
        in_specs=in_specs,
        out_specs=[zb_spec, st_spec],
        out_shape=[jax.ShapeDtypeStruct(zb_all.shape, zb_all.dtype),
                   jax.ShapeDtypeStruct(state.shape, F32)],
        scratch_shapes=[pltpu.VMEM((r, dv), F32)],
        input_output_aliases=aliases,
        compiler_params=_cparams("parallel", "parallel"),
        name="gla_sample",
    )(*args)


def _conv_prompt_kernel(cb_ref, cc_ref, cx_ref, cg_ref, pc_ref, px_ref, w_ref, za_ref, nb_ref, *, tiles_per_seq):
    i = pl.program_id(1)
    tm = cc_ref.shape[0]
    u = cc_ref[...].astype(F32) * cx_ref[...].astype(F32)
    halo = pc_ref[...].astype(F32) * px_ref[...].astype(F32)
    halo = jnp.where(i % tiles_per_seq == 0, 0.0, halo)
    hr = halo.shape[0]
    row = lax.broadcasted_iota(jnp.int32, u.shape, 0)
    um1 = jnp.where(row == 0, halo[hr - 1:hr, :], pltpu.roll(u, 1, 0))
    um2 = jnp.where(row == 0, halo[hr - 2:hr - 1, :],
                    jnp.where(row == 1, halo[hr - 1:hr, :], pltpu.roll(u, 2, 0)))
    w = w_ref[...]
    conv = um2 * w[0:1, :] + um1 * w[1:2, :] + u * w[2:3, :]
    za_ref[...] = _pack_rows(cb_ref[...].astype(F32) * conv * cg_ref[...].astype(F32))
    nb_ref[...] = u[tm - 2:tm, :]


def _conv_prompt(p, conv_w, layer, offs, batch, seq, rows_total, cd, tm, tn):
    tiles_per_seq = seq // tm
    hr = 16
    kern = functools.partial(_conv_prompt_kernel, tiles_per_seq=tiles_per_seq)
    cbo, cco, cxo, cgo = (offs[n] // tn for n in ("cB", "cC", "cx", "cg"))
    prev = lambda i: jnp.maximum(i * (tm // hr) - 1, 0)
    return pl.pallas_call(
        kern,
        grid=(cd // tn, batch * tiles_per_seq),
        in_specs=[pl.BlockSpec((tm, tn), lambda j, i: (i, cbo + j)),
                  pl.BlockSpec((tm, tn), lambda j, i: (i, cco + j)),
                  pl.BlockSpec((tm, tn), lambda j, i: (i, cxo + j)),
                  pl.BlockSpec((tm, tn), lambda j, i: (i, cgo + j)),
                  pl.BlockSpec((hr, tn), lambda j, i: (prev(i), cco + j)),
                  pl.BlockSpec((hr, tn), lambda j, i: (prev(i), cxo + j)),
                  pl.BlockSpec((None, CONV_WIDTH, tn), lambda j, i: (layer, 0, j))],
        out_specs=[pl.BlockSpec((tm // 2, tn), lambda j, i: (i, j)),
                   pl.BlockSpec((None, CONV_WIDTH - 1, tn), lambda j, i: (i // tiles_per_seq, 0, j))],
        out_shape=[jax.ShapeDtypeStruct((rows_total // 2, cd), jnp.uint32),
                   jax.ShapeDtypeStruct((batch, CONV_WIDTH - 1, cd), F32)],
        compiler_params=_cparams("arbitrary", "arbitrary"),
        name="conv_prompt",
    )(p, p, p, p, p, p, conv_w)


def _conv_sample_kernel(cb_ref, cc_ref, cx_ref, cg_ref, st_ref, w_ref, za_all_ref, za_ref, nb_ref):
    del za_all_ref
    r, tn = cc_ref.shape
    nseq = r // SAMPLE_LEN
    u = cc_ref[...].astype(F32) * cx_ref[...].astype(F32)
    st = st_ref[...]
    rows = lambda x: jnp.broadcast_to(x, (nseq, SAMPLE_LEN, tn)).reshape(r, tn)
    s0, s1 = rows(st[:, 0:1, :]), rows(st[:, 1:2, :])
    pos = lax.broadcasted_iota(jnp.int32, u.shape, 0) & (SAMPLE_LEN - 1)
    um1 = jnp.where(pos == 0, s1, pltpu.roll(u, 1, 0))
    um2 = jnp.where(pos == 0, s0, jnp.where(pos == 1, s1, pltpu.roll(u, 2, 0)))
    w = w_ref[...]
    conv = um2 * w[0:1, :] + um1 * w[1:2, :] + u * w[2:3, :]
    za_ref[...] = _pack_rows(cb_ref[...].astype(F32) * conv * cg_ref[...].astype(F32))
    nb_ref[...] = u.reshape(nseq, SAMPLE_LEN, tn)[:, SAMPLE_LEN - 2:SAMPLE_LEN, :]


def _conv_sample(p, conv_w, state, za_all, layer, offs, row0, cd, sb, tn):
    nseq = state.shape[1]
    r = sb * SAMPLE_LEN
    rb0 = row0 // r
    cbo, cco, cxo, cgo = (offs[n] // tn for n in ("cB", "cC", "cx", "cg"))
    return pl.pallas_call(
        _conv_sample_kernel,
        grid=(nseq // sb, cd // tn),
        in_specs=[pl.BlockSpec((r, tn), lambda i, j: (rb0 + i, cbo + j)),
                  pl.BlockSpec((r, tn), lambda i, j: (rb0 + i, cco + j)),
                  pl.BlockSpec((r, tn), lambda i, j: (rb0 + i, cxo + j)),
                  pl.BlockSpec((r, tn), lambda i, j: (rb0 + i, cgo + j)),
                  pl.BlockSpec((None, sb, CONV_WIDTH - 1, tn), lambda i, j: (layer, i, 0, j)),
                  pl.BlockSpec((None, CONV_WIDTH, tn), lambda i, j: (layer, 0, j)),
                  pl.BlockSpec(memory_space=pl.ANY)],
        out_specs=[pl.BlockSpec((r // 2, tn), lambda i, j: (rb0 + i, j)),
                   pl.BlockSpec((sb, CONV_WIDTH - 1, tn), lambda i, j: (i, 0, j))],
        out_shape=[jax.ShapeDtypeStruct(za_all.shape, za_all.dtype),
                   jax.ShapeDtypeStruct(state.shape[1:], F32)],
        input_output_aliases={6: 0},
        compiler_params=_cparams("parallel", "parallel"),
        name="conv_sample",
    )(p, p, p, p, state, conv_w, za_all)


def _branch_kernel(za_ref, zb_ref, wa_ref, wb_ref, sga_ref, sgb_ref, o_ref, wa_bf, wb_bf):
    @pl.when(pl.program_id(1) == 0)
    def _():
        wa_bf[...] = wa_ref[...].astype(BF16)
        wb_bf[...] = wb_ref[...].astype(BF16)

    tn = o_ref.shape[1]
    sub = min(tn, MXU_WIDTH)
    for c in range(tn // sub):
        cols = slice(c * sub, (c + 1) * sub)
        a = _dot(_unpack_rows(za_ref[...]), wa_bf[:, cols])
        b = _dot(_unpack_rows(zb_ref[...]), wb_bf[:, cols])
        o_ref[:, cols] = _pack_rows(sga_ref[:, cols].astype(F32) * a + sgb_ref[:, cols].astype(F32) * b)


def _branch(za, zb, w_a, w_b, p, layer, offs, tm, tn):
    m, cd = 2 * za.shape[0], za.shape[1]
    dvt = zb.shape[1]
    d = w_a.shape[2]
    gao, gbo = offs["ga"] // tn, offs["gb"] // tn
    return pl.pallas_call(
        _branch_kernel,
        grid=(d // tn, m // tm),
        in_specs=[pl.BlockSpec((tm // 2, cd), lambda j, i: (i, 0)),
                  pl.BlockSpec((tm // 2, dvt), lambda j, i: (i, 0)),
                  pl.BlockSpec((None, cd, tn), lambda j, i: (layer, 0, j)),
                  pl.BlockSpec((None, dvt, tn), lambda j, i: (layer, 0, j)),
                  pl.BlockSpec((tm, tn), lambda j, i: (i, gao + j)),
                  pl.BlockSpec((tm, tn), lambda j, i: (i, gbo + j))],
        out_specs=pl.BlockSpec((tm // 2, tn), lambda j, i: (i, j)),
        out_shape=jax.ShapeDtypeStruct((m // 2, d), jnp.uint32),
        scratch_shapes=[pltpu.VMEM((cd, tn), BF16), pltpu.VMEM((dvt, tn), BF16)],
        compiler_params=_cparams("arbitrary", "arbitrary"),
        name="branch",
    )(za, zb, w_a, w_b, p, p)


def _out_kernel(m_ref, w_ref, x_ref, g_ref, xo_ref, ho_ref, w_bf):
    @pl.when(pl.program_id(0) == 0)
    def _():
        w_bf[...] = w_ref[...].astype(BF16)

    x = x_ref[...] + _dot(_unpack_rows(m_ref[...]), w_bf[...])
    xo_ref[...] = x
    ho_ref[...] = _pack_rows(_rms(x, g_ref[...]))


def _out_final_kernel(m_ref, w_ref, x_ref, g_ref, yp_ref, ys_ref, w_bf, *, prompt_tiles):
    i = pl.program_id(0)

    @pl.when(i == 0)
    def _():
        w_bf[...] = w_ref[...].astype(BF16)

    y = _rms(x_ref[...] + _dot(_unpack_rows(m_ref[...]), w_bf[...]), g_ref[...])

    @pl.when(i < prompt_tiles)
    def _():
        yp_ref[...] = y

    @pl.when(i >= prompt_tiles)
    def _():
        ys_ref[...] = y


def _out(merged, w_out, x, g, layer, tm, prompt_rows=None):
    m, d = x.shape
    in_specs = [pl.BlockSpec((tm // 2, d), lambda i: (i, 0)),
                pl.BlockSpec((None, d, d), lambda i: (layer, 0, 0), pipeline_mode=pl.Buffered(1)),
                pl.BlockSpec((tm, d), lambda i: (i, 0)),
                pl.BlockSpec((1, d), lambda i: (0, 0))]
    if prompt_rows is None:
        kern = _out_kernel
        out_specs = [pl.BlockSpec((tm, d), lambda i: (i, 0)), pl.BlockSpec((tm // 2, d), lambda i: (i, 0))]
        out_shape = [jax.ShapeDtypeStruct((m, d), F32), jax.ShapeDtypeStruct((m // 2, d), jnp.uint32)]
    else:
        pt = prompt_rows // tm
        kern = functools.partial(_out_final_kernel, prompt_tiles=pt)
        out_specs = [pl.BlockSpec((tm, d), lambda i: (jnp.minimum(i, pt - 1), 0)),
                     pl.BlockSpec((tm, d), lambda i: (jnp.maximum(i - pt, 0), 0))]
        out_shape = [jax.ShapeDtypeStruct((prompt_rows, d), F32), jax.ShapeDtypeStruct((m - prompt_rows, d), F32)]
    return pl.pallas_call(
        kern,
        grid=(m // tm,),
        in_specs=in_specs,
        out_specs=out_specs,
        out_shape=out_shape,
        scratch_shapes=[pltpu.VMEM((d, d), BF16)],
        compiler_params=_cparams("arbitrary"),
        name="outproj",
    )(merged, w_out, x, g.reshape(1, d))


def kernel(x_prompt, x_sample, state_conv, state_gla, norm_g, w_in, conv_w, w_alpha2, b_alpha,
           gla_norm_g, w_branch_a, w_branch_b, w_out, final_norm_g):
    batch, seq, d = x_prompt.shape
    nsamp, slen, _ = x_sample.shape
    depth, _, _, cd = state_conv.shape
    _, _, heads, dk, dv = state_gla.shape
    rank, dkt = w_alpha2.shape[1:]
    dvt = heads * dv
    assert slen == SAMPLE_LEN and dkt == heads * dk and state_conv.shape[2] == CONV_WIDTH - 1
    widths = (("ga", d), ("gb", d), ("cB", cd), ("cC", cd), ("cx", cd), ("cg", cd),
              ("q", dkt), ("k", dkt), ("v", dvt), ("gg", dvt), ("lr", rank))
    offs, o = {}, 0
    for name, wd in widths:
        offs[name] = o
        o += wd
    assert o == w_in.shape[2] and rank <= LANES and offs["lr"] % LANES == 0

    mp, ms = batch * seq, nsamp * slen
    m = mp + ms
    tm = _tile(ms, 1024)
    tn = _tile(dkt, 1024)
    chunk = _tile(seq, 256)
    sb = _tile(nsamp, 16)
    w2 = jnp.pad(w_alpha2, ((0, 0), (0, LANES - rank), (0, 0))).astype(BF16)
    w_in_t = jnp.swapaxes(w_in, 1, 2)

    x, h = _embed(x_prompt.reshape(mp, d), x_sample.reshape(ms, d), norm_g[0])
    conv_p, gla_p, conv_s, gla_s = [], [], [], None
    for l in range(depth):
        p = _inproj(h, w_in_t, l, offs, dk, tm, tn)
        la = _gate(h, w_in_t, w2, b_alpha, l, offs["lr"], rank)
        zb, s_p = _gla_prompt(p, la, gla_norm_g, l, offs, batch, seq, m, heads, dk, dv, chunk, _tile(seq, 1024))
        zb, gla_s = _gla_sample(p, la, gla_norm_g, state_gla, zb, gla_s, l, offs, mp, sb)
        za, nb_p = _conv_prompt(p, conv_w, l, offs, batch, seq, m, cd, _tile(seq, 256), _tile(cd, 1024))
        za, nb_s = _conv_sample(p, conv_w, state_conv, za, l, offs, mp, cd, sb, _tile(cd, 1024))
        merged = _branch(za, zb, w_branch_a, w_branch_b, p, l, offs, tm, _tile(d, 512))
        conv_p.append(nb_p)
        gla_p.append(s_p)
        conv_s.append(nb_s)
        if l < depth - 1:
            x, h = _out(merged, w_out, x, norm_g[l + 1], l, _tile(ms, 256))
        else:
            y_p, y_s = _out(merged, w_out, x, final_norm_g, l, _tile(ms, 256), prompt_rows=mp)
    return (y_p.reshape(batch, seq, d), y_s.reshape(nsamp, slen, d),
            jnp.stack(conv_p), jnp.stack(gla_p), jnp.stack(conv_s), gla_s)
```

```python
import functools

import jax
import jax.numpy as jnp
from jax import lax
from jax.experimental import pallas as pl
from jax.experimental.pallas import tpu as pltpu

F32 = jnp.float32
BF16 = jnp.bfloat16
NORM_EPS = 1e-6
GLA_TAU = 16.0
LOG2E = 1.4426950408889634
CONV_WIDTH = 3
SAMPLE_LEN = 8
LANES = 128
MXU_WIDTH = 256
VMEM_LIMIT = 56 << 20


def _cparams(*sem):
    return pltpu.CompilerParams(dimension_semantics=sem, vmem_limit_bytes=VMEM_LIMIT)


def _dot(a, b):
    return jnp.dot(a, b, preferred_element_type=F32)


def _dot_nt(a, b):
    return lax.dot_general(a, b, (((1,), (1,)), ((), ())), preferred_element_type=F32)


def _split3(x):
    hi = x.astype(BF16)
    r1 = x - hi.astype(F32)
    mid = r1.astype(BF16)
    lo = (r1 - mid.astype(F32)).astype(BF16)
    return hi, mid, lo


def _tile(n, pref):
    t = min(n, pref)
    while n % t:
        t //= 2
    return t


def _pack_rows(x):
    return pltpu.bitcast(x.astype(BF16), jnp.uint32)


def _unpack_rows(x):
    return pltpu.bitcast(x, BF16)


def _sigmoid(x):
    return 0.5 * jnp.tanh(0.5 * x) + 0.5


def _rms(x, g):
    ms = jnp.mean(x * x, axis=-1, keepdims=True)
    return x * lax.rsqrt(ms + NORM_EPS) * g


def _embed_kernel(xp_ref, xs_ref, g_ref, x_ref, h_ref, *, prompt_tiles):
    i = pl.program_id(0)

    def emit(x):
        x_ref[...] = x
        h_ref[...] = _pack_rows(_rms(x, g_ref[...]))

    @pl.when(i < prompt_tiles)
    def _():
        emit(xp_ref[...])

    @pl.when(i >= prompt_tiles)
    def _():
        emit(xs_ref[...])


def _embed(xp, xs, g):
    (mp, d), ms = xp.shape, xs.shape[0]
    tm = _tile(ms, 512)
    pt = mp // tm
    kern = functools.partial(_embed_kernel, prompt_tiles=pt)
    return pl.pallas_call(
        kern,
        grid=((mp + ms) // tm,),
        in_specs=[pl.BlockSpec((tm, d), lambda i: (jnp.minimum(i, pt - 1), 0)),
                  pl.BlockSpec((tm, d), lambda i: (jnp.maximum(i - pt, 0), 0)),
                  pl.BlockSpec((1, d), lambda i: (0, 0))],
        out_specs=[pl.BlockSpec((tm, d), lambda i: (i, 0)),
                   pl.BlockSpec((tm // 2, d), lambda i: (i, 0))],
        out_shape=[jax.ShapeDtypeStruct((mp + ms, d), F32),
                   jax.ShapeDtypeStruct(((mp + ms) // 2, d), jnp.uint32)],
        compiler_params=_cparams("arbitrary"),
        name="embed",
    )(xp, xs, g.reshape(1, d))


def _inproj_kernel(h_ref, w_ref, o_ref, wb_ref, *, sig_end, silu_lo, q_lo, q_hi, q_scale):
    j = pl.program_id(0)

    @pl.when(pl.program_id(1) == 0)
    def _():
        wb_ref[...] = w_ref[...].astype(BF16)

    is_sig = j < sig_end
    is_silu = j >= silu_lo
    is_q = (j >= q_lo) & (j < q_hi)
    gated = is_sig | is_silu
    alpha = jnp.where(gated, 0.5, jnp.where(is_q, q_scale, 1.0)).astype(F32)
    beta = jnp.where(gated, 0.5, 0.0).astype(F32)
    tm, tn = o_ref.shape
    sub_n = min(tn, 2 * MXU_WIDTH)
    sub_m = min(tm, 512)
    for c in range(tn // sub_n):
        cols = slice(c * sub_n, (c + 1) * sub_n)
        for r in range(tm // sub_m):
            acc = _dot_nt(_unpack_rows(h_ref[r * sub_m // 2:(r + 1) * sub_m // 2, :]), wb_ref[cols, :])
            gate = alpha + beta * jnp.tanh(0.5 * acc)
            o_ref[r * sub_m:(r + 1) * sub_m, cols] = (jnp.where(is_sig, 1.0, acc) * gate).astype(o_ref.dtype)


def _inproj(h, w_in_t, layer, woffs, poffs, dk, tm, tn):
    m, d = 2 * h.shape[0], h.shape[1]
    n = poffs["end"]
    gate_tiles = poffs["q"] // tn
    skip_tiles = (woffs["q"] - woffs["cB"]) // tn
    kern = functools.partial(
        _inproj_kernel,
        sig_end=gate_tiles, silu_lo=poffs["gg"] // tn,
        q_lo=poffs["q"] // tn, q_hi=poffs["k"] // tn, q_scale=float(dk) ** -0.5)
    w_tile = lambda j: jnp.where(j < gate_tiles, j, j + skip_tiles)
    return pl.pallas_call(
        kern,
        grid=(n // tn, m // tm),
        in_specs=[pl.BlockSpec((tm // 2, d), lambda j, i: (i, 0)),
                  pl.BlockSpec((None, tn, d), lambda j, i: (layer, w_tile(j), 0))],
        out_specs=pl.BlockSpec((tm, tn), lambda j, i: (i, j)),
        out_shape=jax.ShapeDtypeStruct((m, n), BF16),
        scratch_shapes=[pltpu.VMEM((tn, d), BF16)],
        compiler_params=_cparams("arbitrary", "arbitrary"),
        name="inproj",
    )(h, w_in_t)


def _gate_kernel(h_ref, wlr_ref, w2_ref, ba_ref, o_ref, *, rank):
    lr = _dot_nt(_unpack_rows(h_ref[...]), wlr_ref[...].astype(BF16))
    lr = jnp.where(lax.broadcasted_iota(jnp.int32, lr.shape, 1) < rank, lr, 0.0)
    logit = _dot(lr.astype(BF16), w2_ref[...]) + ba_ref[...]
    log_sig = jnp.minimum(logit, 0.0) - jnp.log1p(jnp.exp(-jnp.abs(logit)))
    o_ref[...] = log_sig / GLA_TAU


def _gate(h, w_in_t, w2, ba, layer, off_lr, rank):
    m, d = 2 * h.shape[0], h.shape[1]
    rp, dkt = w2.shape[1:]
    tm = _tile(m, 512)
    return pl.pallas_call(
        functools.partial(_gate_kernel, rank=rank),
        grid=(m // tm,),
        in_specs=[pl.BlockSpec((tm // 2, d), lambda i: (i, 0)),
                  pl.BlockSpec((None, rp, d), lambda i: (layer, off_lr // rp, 0)),
                  pl.BlockSpec((None, rp, dkt), lambda i: (layer, 0, 0)),
                  pl.BlockSpec((None, 1, dkt), lambda i: (layer, 0, 0))],
        out_specs=pl.BlockSpec((tm, dkt), lambda i: (i, 0)),
        out_shape=jax.ShapeDtypeStruct((m, dkt), F32),
        compiler_params=_cparams("parallel"),
        name="gate",
    )(h, w_in_t, w2, ba.reshape(ba.shape[0], 1, dkt))


def _block_row(x, period, idx):
    c, w = x.shape
    x3 = x.reshape(c // period, period, w)
    return jnp.broadcast_to(x3[:, idx:idx + 1, :], x3.shape).reshape(c, w)


def _level_factor(q, k, b2, la2, m):
    c, w = b2.shape
    pos = lax.broadcasted_iota(jnp.int32, (c, w), 0) & (2 * m - 1)
    upper = pos >= m
    if 2 * m >= 8:
        e = -jnp.abs(b2 - _block_row(b2, 2 * m, m - 1))
    elif m == 1:
        e = jnp.where(upper, la2, 0.0)
    else:
        assert m == 2
        la_prev = pltpu.roll(la2, 1, 0)
        la_next = pltpu.roll(la2, c - 1, 0)
        e = jnp.where(pos == 0, la_next, jnp.where(pos == 1, 0.0, jnp.where(pos == 2, la2, la2 + la_prev)))
    return (jnp.where(upper, q, k) * jnp.exp2(e)).astype(BF16)


def _pair_code(n):
    row = lax.broadcasted_iota(jnp.int32, (n, n), 0)
    col = lax.broadcasted_iota(jnp.int32, (n, n), 1)
    xf = (row ^ col).astype(F32)
    high_bit = (lax.bitcast_convert_type(xf, jnp.int32) >> 23) - 127
    return jnp.where(col < row, high_bit, jnp.where(col == row, -1, -2))


def _diag_scores(q, k, b2, la2, levels, code, blocks):
    qb, kb = q.astype(BF16), k.astype(BF16)
    out = [jnp.where(code == -1, _dot_nt(qb[lo:hi], kb[lo:hi]), 0.0) for lo, hi in blocks]
    for m in levels:
        g = _level_factor(q, k, b2, la2, m)
        bit = m.bit_length() - 1
        out = [jnp.where(code == bit, _dot_nt(g[lo:hi], g[lo:hi]), a) for a, (lo, hi) in zip(out, blocks)]
    return out


def _head_norm_gate(o, gn, gg):
    return _rms(o, gn) * gg


def _levels(c):
    out, m = [], c // 2
    while m >= 1:
        out.append(m)
        m //= 2
    return tuple(out)


def _gla_prompt_kernel(q_ref, k_ref, v_ref, gg_ref, la_ref, gn_ref, zb_ref, s_ref, *, chunk):
    dk = q_ref.shape[1]
    c, hc = chunk, chunk // 2

    @pl.when(pl.program_id(2) == 0)
    def _():
        s_ref[...] = jnp.zeros_like(s_ref)

    def step(ci, carry):
        rows = pl.ds(pl.multiple_of(ci * c, c), c)
        la = la_ref[rows, :]
        row = lax.broadcasted_iota(jnp.int32, (c, c), 0)
        col = lax.broadcasted_iota(jnp.int32, (c, c), 1)
        tril = (col <= row).astype(BF16)
        hi, mid, lo = _split3(la)
        b2 = (_dot(tril, hi) + _dot(tril, mid) + _dot(tril, lo)) * LOG2E
        la2 = la * LOG2E
        b2_last = b2[c - 1:c, :]

        q = q_ref[rows, :].astype(F32)
        k = k_ref[rows, :].astype(F32)
        v = v_ref[rows, :]
        levels = _levels(c)
        a00, a11 = _diag_scores(q, k, b2, la2, levels[1:], _pair_code(hc), ((0, hc), (hc, c)))
        g_top = _level_factor(q, k, b2, la2, hc)
        a10 = _dot_nt(g_top[hc:], g_top[:hc])
        s_old = s_ref[...]
        o_intra = jnp.concatenate(
            [_dot(a00.astype(BF16), v[:hc]),
             _dot(jnp.concatenate([a10, a11], axis=1).astype(BF16), v)], axis=0)
        o = o_intra + _dot((q * jnp.exp2(b2)).astype(BF16), s_old.astype(BF16))
        packed_rows = pl.ds(pl.multiple_of(ci * hc, hc), hc)
        zb_ref[packed_rows, :] = _pack_rows(_head_norm_gate(o, gn_ref[...], gg_ref[rows, :].astype(F32)))

        kd_t = (k * jnp.exp2(b2_last - b2)).T.astype(BF16)
        eye = lax.broadcasted_iota(jnp.int32, (dk, dk), 0) == lax.broadcasted_iota(jnp.int32, (dk, dk), 1)
        d_col = jnp.sum(jnp.where(eye, jnp.exp2(b2_last), 0.0), axis=1, keepdims=True)
        s_ref[...] = d_col * s_old + _dot(kd_t, v)
        return carry

    lax.fori_loop(0, q_ref.shape[0] // c, step, 0)


def _gla_prompt(p, la, gn, layer, offs, batch, seq, rows_total, heads, dk, dv, chunk, tblk):
    nb = seq // tblk
    kern = functools.partial(_gla_prompt_kernel, chunk=chunk)
    qb, kb, vb, gb = offs["q"] // dk, offs["k"] // dk, offs["v"] // dv, offs["gg"] // dv
    rows = lambda b, h, n: b * nb + n
    return pl.pallas_call(
        kern,
        grid=(batch, heads, nb),
        in_specs=[pl.BlockSpec((tblk, dk), lambda b, h, n: (rows(b, h, n), qb + h)),
                  pl.BlockSpec((tblk, dk), lambda b, h, n: (rows(b, h, n), kb + h)),
                  pl.BlockSpec((tblk, dv), lambda b, h, n: (rows(b, h, n), vb + h)),
                  pl.BlockSpec((tblk, dv), lambda b, h, n: (rows(b, h, n), gb + h)),
                  pl.BlockSpec((tblk, dk), lambda b, h, n: (rows(b, h, n), h)),
                  pl.BlockSpec((None, 1, dv), lambda b, h, n: (layer, 0, 0))],
        out_specs=[pl.BlockSpec((tblk // 2, dv), lambda b, h, n: (rows(b, h, n), h)),
                   pl.BlockSpec((None, None, dk, dv), lambda b, h, n: (b, h, 0, 0))],
        out_shape=[jax.ShapeDtypeStruct((rows_total // 2, heads * dv), jnp.uint32),
                   jax.ShapeDtypeStruct((batch, heads, dk, dv), F32)],
        compiler_params=_cparams("parallel", "parallel", "arbitrary"),
        name="gla_prompt",
    )(p, p, p, p, la, gn.reshape(gn.shape[0], 1, dv))


def _gla_sample_body(q_ref, k_ref, v_ref, gg_ref, la_ref, gn_ref, sin_ref, zb_ref, sout_ref, o_scr):
    r, dk = q_ref.shape
    nseq = r // SAMPLE_LEN
    la = la_ref[...]
    row = lax.broadcasted_iota(jnp.int32, (r, r), 0)
    col = lax.broadcasted_iota(jnp.int32, (r, r), 1)
    tril = ((col <= row) & ((row ^ col) < SAMPLE_LEN)).astype(BF16)
    hi, mid, lo = _split3(la)
    b2 = (_dot(tril, hi) + _dot(tril, mid) + _dot(tril, lo)) * LOG2E
    b2_last = _block_row(b2, SAMPLE_LEN, SAMPLE_LEN - 1)

    q = q_ref[...].astype(F32)
    k = k_ref[...].astype(F32)
    v = v_ref[...]
    (a,) = _diag_scores(q, k, b2, la * LOG2E, _levels(SAMPLE_LEN), _pair_code(r), ((0, r),))
    o_scr[...] = _dot(a.astype(BF16), v)
    decay = jnp.exp2(b2)
    qe = (q * decay).astype(BF16)
    kd_t = (k * jnp.exp2(b2_last - b2)).T
    d_t = decay.T
    lane_seq = lax.broadcasted_iota(jnp.int32, (dk, r), 1) // SAMPLE_LEN
    for s in range(nseq):
        lo_r, hi_r = s * SAMPLE_LEN, (s + 1) * SAMPLE_LEN
        s_old = sin_ref[s]
        o_scr[lo_r:hi_r, :] += _dot(qe[lo_r:hi_r, :], s_old.astype(BF16))
        kd_s = jnp.where(lane_seq == s, kd_t, 0.0).astype(BF16)
        sout_ref[s] = d_t[:, hi_r - 1:hi_r] * s_old + _dot(kd_s, v)
    zb_ref[...] = _pack_rows(_head_norm_gate(o_scr[...], gn_ref[...], gg_ref[...].astype(F32)))


def _gla_sample_kernel(n_alias, *refs):
    _gla_sample_body(*refs[:7], *refs[7 + n_alias:])


def _gla_sample(p, la, gn, state, zb_all, new_state, layer, offs, row0, sb):
    _, nseq, heads, dk, dv = state.shape
    r = sb * SAMPLE_LEN
    rb0 = row0 // r
    qb, kb, vb, gb = offs["q"] // dk, offs["k"] // dk, offs["v"] // dv, offs["gg"] // dv
    st_spec = pl.BlockSpec((None, sb, None, dk, dv), lambda i, h: (layer, i, h, 0, 0))
    zb_spec = pl.BlockSpec((r // 2, dv), lambda i, h: (rb0 + i, h))
    in_specs = [pl.BlockSpec((r, dk), lambda i, h: (rb0 + i, qb + h)),
                pl.BlockSpec((r, dk), lambda i, h: (rb0 + i, kb + h)),
                pl.BlockSpec((r, dv), lambda i, h: (rb0 + i, vb + h)),
                pl.BlockSpec((r, dv), lambda i, h: (rb0 + i, gb + h)),
                pl.BlockSpec((r, dk), lambda i, h: (rb0 + i, h)),
                pl.BlockSpec((None, 1, dv), lambda i, h: (layer, 0, 0)),
                st_spec,
                pl.BlockSpec(memory_space=pl.ANY)]
    args = [p, p, p, p, la, gn.reshape(gn.shape[0], 1, dv), state, zb_all]
    aliases = {7: 0}
    if new_state is not None:
        in_specs.append(pl.BlockSpec(memory_space=pl.ANY))
        args.append(new_state)
        aliases[8] = 1
    return pl.pallas_call(
        functools.partial(_gla_sample_kernel, len(aliases)),
        grid=(nseq // sb, heads),
        in_specs=in_specs,
        out_specs=[zb_spec, st_spec],
        out_shape=[jax.ShapeDtypeStruct(zb_all.shape, zb_all.dtype),
                   jax.ShapeDtypeStruct(state.shape, F32)],
        scratch_shapes=[pltpu.VMEM((r, dv), F32)],
        input_output_aliases=aliases,
        compiler_params=_cparams("parallel", "parallel"),
        name="gla_sample",
    )(*args)


def _short_conv(u, um1, um2, w):
    return um2 * w[0:1, :] + um1 * w[1:2, :] + u * w[2:3, :]


def _inproj_conv_kernel(h_ref, wcb_ref, wcc_ref, wcx_ref, wcg_ref, cw_ref, st_ref, za_ref, nbp_ref, nbs_ref,
                        wb_ref, halo_ref, *, prompt_tiles, tiles_per_seq, sub_m):
    i = pl.program_id(1)
    tc = wcb_ref.shape[0]
    tm = 2 * h_ref.shape[0]

    @pl.when(i == 0)
    def _():
        for g, ref in enumerate((wcb_ref, wcc_ref, wcx_ref, wcg_ref)):
            wb_ref[g * tc:(g + 1) * tc, :] = ref[...].astype(BF16)

    w = cw_ref[...]

    def project(r):
        hh = _unpack_rows(h_ref[r * sub_m // 2:(r + 1) * sub_m // 2, :])
        bc = _dot_nt(hh, wb_ref[0:2 * tc, :])
        xg = _dot_nt(hh, wb_ref[2 * tc:4 * tc, :])
        cg = xg[:, tc:]
        return bc[:, :tc] * (cg * _sigmoid(cg)), bc[:, tc:] * xg[:, :tc]

    @pl.when(i < prompt_tiles)
    def _():
        @pl.when(i % tiles_per_seq == 0)
        def _():
            halo_ref[...] = jnp.zeros_like(halo_ref)

        for r in range(tm // sub_m):
            gate, u = project(r)
            hr = halo_ref.shape[0]
            prev = halo_ref[...]
            row = lax.broadcasted_iota(jnp.int32, u.shape, 0)
            um1 = jnp.where(row == 0, prev[hr - 1:hr, :], pltpu.roll(u, 1, 0))
            um2 = jnp.where(row == 0, prev[hr - 2:hr - 1, :],
                            jnp.where(row == 1, prev[hr - 1:hr, :], pltpu.roll(u, 2, 0)))
            za_ref[r * sub_m // 2:(r + 1) * sub_m // 2, :] = _pack_rows(gate * _short_conv(u, um1, um2, w))
            halo_ref[...] = u[sub_m - hr:sub_m, :]
        nbp_ref[...] = u[sub_m - (CONV_WIDTH - 1):sub_m, :]

    @pl.when(i >= prompt_tiles)
    def _():
        nsq = sub_m // SAMPLE_LEN
        for r in range(tm // sub_m):
            gate, u = project(r)
            st = st_ref[r * nsq:(r + 1) * nsq]
            rows = lambda x: jnp.broadcast_to(x, (nsq, SAMPLE_LEN, tc)).reshape(sub_m, tc)
            s0, s1 = rows(st[:, 0:1, :]), rows(st[:, 1:2, :])
            pos = lax.broadcasted_iota(jnp.int32, u.shape, 0) & (SAMPLE_LEN - 1)
            um1 = jnp.where(pos == 0, s1, pltpu.roll(u, 1, 0))
            um2 = jnp.where(pos == 0, s0, jnp.where(pos == 1, s1, pltpu.roll(u, 2, 0)))
            za_ref[r * sub_m // 2:(r + 1) * sub_m // 2, :] = _pack_rows(gate * _short_conv(u, um1, um2, w))
            nbs_ref[r * nsq:(r + 1) * nsq] = u.reshape(nsq, SAMPLE_LEN, tc)[:, SAMPLE_LEN - (CONV_WIDTH - 1):, :]


def _inproj_conv(h, w_in_t, conv_w, state, layer, woffs, batch, seq, tm, tc):
    m, d = 2 * h.shape[0], h.shape[1]
    nsamp, cd = state.shape[1], state.shape[3]
    pt = batch * seq // tm
    tiles_per_seq = seq // tm
    sps = tm // SAMPLE_LEN
    sub_m = min(tm, 512)
    kern = functools.partial(_inproj_conv_kernel, prompt_tiles=pt, tiles_per_seq=tiles_per_seq, sub_m=sub_m)
    w_spec = lambda name: pl.BlockSpec((None, tc, d), lambda j, i: (layer, woffs[name] // tc + j, 0))
    samp = lambda i: jnp.maximum(i - pt, 0)
    return pl.pallas_call(
        kern,
        grid=(cd // tc, m // tm),
        in_specs=[pl.BlockSpec((tm // 2, d), lambda j, i: (i, 0)),
                  w_spec("cB"), w_spec("cC"), w_spec("cx"), w_spec("cg"),
                  pl.BlockSpec((None, CONV_WIDTH, tc), lambda j, i: (layer, 0, j)),
                  pl.BlockSpec((None, sps, CONV_WIDTH - 1, tc), lambda j, i: (layer, samp(i), 0, j))],
        out_specs=[pl.BlockSpec((tm // 2, tc), lambda j, i: (i, j)),
                   pl.BlockSpec((None, CONV_WIDTH - 1, tc),
                                lambda j, i: (jnp.minimum(i // tiles_per_seq, batch - 1), 0, j)),
                   pl.BlockSpec((sps, CONV_WIDTH - 1, tc), lambda j, i: (samp(i), 0, j))],
        out_shape=[jax.ShapeDtypeStruct((m // 2, cd), jnp.uint32),
                   jax.ShapeDtypeStruct((batch, CONV_WIDTH - 1, cd), F32),
                   jax.ShapeDtypeStruct((nsamp, CONV_WIDTH - 1, cd), F32)],
        scratch_shapes=[pltpu.VMEM((4 * tc, d), BF16), pltpu.VMEM((8, tc), F32)],
        compiler_params=_cparams("arbitrary", "arbitrary"),
        name="inproj_conv",
    )(h, w_in_t, w_in_t, w_in_t, w_in_t, conv_w, state)


def _branch_kernel(za_ref, zb_ref, wa_ref, wb_ref, sga_ref, sgb_ref, o_ref, wa_bf, wb_bf):
    @pl.when(pl.program_id(1) == 0)
    def _():
        wa_bf[...] = wa_ref[...].astype(BF16)
        wb_bf[...] = wb_ref[...].astype(BF16)

    tn = o_ref.shape[1]
    sub = min(tn, MXU_WIDTH)
    for c in range(tn // sub):
        cols = slice(c * sub, (c + 1) * sub)
        a = _dot(_unpack_rows(za_ref[...]), wa_bf[:, cols])
        b = _dot(_unpack_rows(zb_ref[...]), wb_bf[:, cols])
        o_ref[:, cols] = _pack_rows(sga_ref[:, cols].astype(F32) * a + sgb_ref[:, cols].astype(F32) * b)


def _branch(za, zb, w_a, w_b, p, layer, offs, tm, tn):
    m, cd = 2 * za.shape[0], za.shape[1]
    dvt = zb.shape[1]
    d = w_a.shape[2]
    gao, gbo = offs["ga"] // tn, offs["gb"] // tn
    return pl.pallas_call(
        _branch_kernel,
        grid=(d // tn, m // tm),
        in_specs=[pl.BlockSpec((tm // 2, cd), lambda j, i: (i, 0)),
                  pl.BlockSpec((tm // 2, dvt), lambda j, i: (i, 0)),
                  pl.BlockSpec((None, cd, tn), lambda j, i: (layer, 0, j)),
                  pl.BlockSpec((None, dvt, tn), lambda j, i: (layer, 0, j)),
                  pl.BlockSpec((tm, tn), lambda j, i: (i, gao + j)),
                  pl.BlockSpec((tm, tn), lambda j, i: (i, gbo + j))],
        out_specs=pl.BlockSpec((tm // 2, tn), lambda j, i: (i, j)),
        out_shape=jax.ShapeDtypeStruct((m // 2, d), jnp.uint32),
        scratch_shapes=[pltpu.VMEM((cd, tn), BF16), pltpu.VMEM((dvt, tn), BF16)],
        compiler_params=_cparams("arbitrary", "arbitrary"),
        name="branch",
    )(za, zb, w_a, w_b, p, p)


def _out_kernel(m_ref, w_ref, x_ref, g_ref, xo_ref, ho_ref, w_bf):
    @pl.when(pl.program_id(0) == 0)
    def _():
        w_bf[...] = w_ref[...].astype(BF16)

    x = x_ref[...] + _dot(_unpack_rows(m_ref[...]), w_bf[...])
    xo_ref[...] = x
    ho_ref[...] = _pack_rows(_rms(x, g_ref[...]))


def _out_final_kernel(m_ref, w_ref, x_ref, g_ref, yp_ref, ys_ref, w_bf, *, prompt_tiles):
    i = pl.program_id(0)

    @pl.when(i == 0)
    def _():
        w_bf[...] = w_ref[...].astype(BF16)

    y = _rms(x_ref[...] + _dot(_unpack_rows(m_ref[...]), w_bf[...]), g_ref[...])

    @pl.when(i < prompt_tiles)
    def _():
        yp_ref[...] = y

    @pl.when(i >= prompt_tiles)
    def _():
        ys_ref[...] = y


def _out(merged, w_out, x, g, layer, tm, prompt_rows=None):
    m, d = x.shape
    in_specs = [pl.BlockSpec((tm // 2, d), lambda i: (i, 0)),
                pl.BlockSpec((None, d, d), lambda i: (layer, 0, 0), pipeline_mode=pl.Buffered(1)),
                pl.BlockSpec((tm, d), lambda i: (i, 0)),
                pl.BlockSpec((1, d), lambda i: (0, 0))]
    if prompt_rows is None:
        kern = _out_kernel
        out_specs = [pl.BlockSpec((tm, d), lambda i: (i, 0)), pl.BlockSpec((tm // 2, d), lambda i: (i, 0))]
        out_shape = [jax.ShapeDtypeStruct((m, d), F32), jax.ShapeDtypeStruct((m // 2, d), jnp.uint32)]
    else:
        pt = prompt_rows // tm
        kern = functools.partial(_out_final_kernel, prompt_tiles=pt)
        out_specs = [pl.BlockSpec((tm, d), lambda i: (jnp.minimum(i, pt - 1), 0)),
                     pl.BlockSpec((tm, d), lambda i: (jnp.maximum(i - pt, 0), 0))]
        out_shape = [jax.ShapeDtypeStruct((prompt_rows, d), F32), jax.ShapeDtypeStruct((m - prompt_rows, d), F32)]
    return pl.pallas_call(
        kern,
        grid=(m // tm,),
        in_specs=in_specs,
        out_specs=out_specs,
        out_shape=out_shape,
        scratch_shapes=[pltpu.VMEM((d, d), BF16)],
        compiler_params=_cparams("arbitrary"),
        name="outproj",
    )(merged, w_out, x, g.reshape(1, d))


def kernel(x_prompt, x_sample, state_conv, state_gla, norm_g, w_in, conv_w, w_alpha2, b_alpha,
           gla_norm_g, w_branch_a, w_branch_b, w_out, final_norm_g):
    batch, seq, d = x_prompt.shape
    nsamp, slen, _ = x_sample.shape
    depth, _, _, cd = state_conv.shape
    _, _, heads, dk, dv = state_gla.shape
    rank, dkt = w_alpha2.shape[1:]
    dvt = heads * dv
    assert slen == SAMPLE_LEN and dkt == heads * dk and state_conv.shape[2] == CONV_WIDTH - 1
    widths = (("ga", d), ("gb", d), ("cB", cd), ("cC", cd), ("cx", cd), ("cg", cd),
              ("q", dkt), ("k", dkt), ("v", dvt), ("gg", dvt), ("lr", rank))
    woffs, o = {}, 0
    for name, wd in widths:
        woffs[name] = o
        o += wd
    assert o == w_in.shape[2] and rank <= LANES and woffs["lr"] % LANES == 0 and cd == d
    poffs, o = {}, 0
    for name, wd in widths:
        if name in ("ga", "gb", "q", "k", "v", "gg"):
            poffs[name] = o
            o += wd
    poffs["end"] = o

    mp, ms = batch * seq, nsamp * slen
    m = mp + ms
    tm = _tile(ms, 1024)
    tn = _tile(dkt, 1024)
    chunk = _tile(seq, 256)
    sb = _tile(nsamp, 16)
    w2 = jnp.pad(w_alpha2, ((0, 0), (0, LANES - rank), (0, 0))).astype(BF16)
    w_in_t = jnp.swapaxes(w_in, 1, 2)

    x, h = _embed(x_prompt.reshape(mp, d), x_sample.reshape(ms, d), norm_g[0])
    conv_p, gla_p, conv_s, gla_s = [], [], [], None
    for l in range(depth):
        p = _inproj(h, w_in_t, l, woffs, poffs, dk, tm, tn)
        za, nb_p, nb_s = _inproj_conv(h, w_in_t, conv_w, state_conv, l, woffs, batch, seq, tm, _tile(cd, MXU_WIDTH))
        la = _gate(h, w_in_t, w2, b_alpha, l, woffs["lr"], rank)
        zb, s_p = _gla_prompt(p, la, gla_norm_g, l, poffs, batch, seq, m, heads, dk, dv, chunk, _tile(seq, 1024))
        zb, gla_s = _gla_sample(p, la, gla_norm_g, state_gla, zb, gla_s, l, poffs, mp, sb)
        merged = _branch(za, zb, w_branch_a, w_branch_b, p, l, poffs, tm, _tile(d, 512))
        conv_p.append(nb_p)
        gla_p.append(s_p)
        conv_s.append(nb_s)
        if l < depth - 1:
            x, h = _out(merged, w_out, x, norm_g[l + 1], l, _tile(ms, 256))
        else:
            y_p, y_s = _out(merged, w_out, x, final_norm_g, l, _tile(ms, 256), prompt_rows=mp)
    return (y_p.reshape(batch, seq, d), y_s.reshape(nsamp, slen, d),
            jnp.stack(conv_p), jnp.stack(gla_p), jnp.stack(conv_s), gla_s)
```

```python
import functools

import jax
import jax.numpy as jnp
from jax import lax
from jax.experimental import pallas as pl
from jax.experimental.pallas import tpu as pltpu

F32 = jnp.float32
BF16 = jnp.bfloat16
NORM_EPS = 1e-6
GLA_TAU = 16.0
LOG2E = 1.4426950408889634
CONV_WIDTH = 3
SAMPLE_LEN = 8
LANES = 128
MXU_WIDTH = 256
VMEM_LIMIT = 56 << 20


def _cparams(*sem):
    return pltpu.CompilerParams(dimension_semantics=sem, vmem_limit_bytes=VMEM_LIMIT)


def _dot(a, b):
    return jnp.dot(a, b, preferred_element_type=F32)


def _dot_nt(a, b):
    return lax.dot_general(a, b, (((1,), (1,)), ((), ())), preferred_element_type=F32)


def _split3(x):
    hi = x.astype(BF16)
    r1 = x - hi.astype(F32)
    mid = r1.astype(BF16)
    lo = (r1 - mid.astype(F32)).astype(BF16)
    return hi, mid, lo


def _tile(n, pref):
    t = min(n, pref)
    while n % t:
        t //= 2
    return t


def _pack_rows(x):
    return pltpu.bitcast(x.astype(BF16), jnp.uint32)


def _unpack_rows(x):
    return pltpu.bitcast(x, BF16)


def _sigmoid(x):
    return 0.5 * jnp.tanh(0.5 * x) + 0.5


def _rms(x, g):
    ms = jnp.mean(x * x, axis=-1, keepdims=True)
    return x * lax.rsqrt(ms + NORM_EPS) * g


def _embed_kernel(xp_ref, xs_ref, g_ref, x_ref, h_ref, *, prompt_tiles):
    i = pl.program_id(0)

    def emit(x):
        x_ref[...] = x
        h_ref[...] = _pack_rows(_rms(x, g_ref[...]))

    @pl.when(i < prompt_tiles)
    def _():
        emit(xp_ref[...])

    @pl.when(i >= prompt_tiles)
    def _():
        emit(xs_ref[...])


def _embed(xp, xs, g):
    (mp, d), ms = xp.shape, xs.shape[0]
    tm = _tile(ms, 512)
    pt = mp // tm
    kern = functools.partial(_embed_kernel, prompt_tiles=pt)
    return pl.pallas_call(
        kern,
        grid=((mp + ms) // tm,),
        in_specs=[pl.BlockSpec((tm, d), lambda i: (jnp.minimum(i, pt - 1), 0)),
                  pl.BlockSpec((tm, d), lambda i: (jnp.maximum(i - pt, 0), 0)),
                  pl.BlockSpec((1, d), lambda i: (0, 0))],
        out_specs=[pl.BlockSpec((tm, d), lambda i: (i, 0)),
                   pl.BlockSpec((tm // 2, d), lambda i: (i, 0))],
        out_shape=[jax.ShapeDtypeStruct((mp + ms, d), F32),
                   jax.ShapeDtypeStruct(((mp + ms) // 2, d), jnp.uint32)],
        compiler_params=_cparams("arbitrary"),
        name="embed",
    )(xp, xs, g.reshape(1, d))


def _inproj_kernel(h_ref, w_ref, o_ref, wb_ref, *, sig_end, silu_lo, q_lo, q_hi, q_scale):
    j = pl.program_id(0)

    @pl.when(pl.program_id(1) == 0)
    def _():
        wb_ref[...] = w_ref[...].astype(BF16)

    is_sig = j < sig_end
    is_silu = j >= silu_lo
    is_q = (j >= q_lo) & (j < q_hi)
    gated = is_sig | is_silu
    alpha = jnp.where(gated, 0.5, jnp.where(is_q, q_scale, 1.0)).astype(F32)
    beta = jnp.where(gated, 0.5, 0.0).astype(F32)
    tm, tn = o_ref.shape
    sub_n = min(tn, 2 * MXU_WIDTH)
    sub_m = min(tm, 1024)
    for c in range(tn // sub_n):
        cols = slice(c * sub_n, (c + 1) * sub_n)
        for r in range(tm // sub_m):
            acc = _dot_nt(_unpack_rows(h_ref[r * sub_m // 2:(r + 1) * sub_m // 2, :]), wb_ref[cols, :])
            gate = alpha + beta * jnp.tanh(0.5 * acc)
            o_ref[r * sub_m:(r + 1) * sub_m, cols] = (jnp.where(is_sig, 1.0, acc) * gate).astype(o_ref.dtype)


def _inproj(h, w_in_t, layer, woffs, poffs, dk, tm, tn):
    m, d = 2 * h.shape[0], h.shape[1]
    n = poffs["end"]
    gate_tiles = poffs["q"] // tn
    skip_tiles = (woffs["q"] - woffs["cB"]) // tn
    kern = functools.partial(
        _inproj_kernel,
        sig_end=gate_tiles, silu_lo=poffs["gg"] // tn,
        q_lo=poffs["q"] // tn, q_hi=poffs["k"] // tn, q_scale=float(dk) ** -0.5)
    w_tile = lambda j: jnp.where(j < gate_tiles, j, j + skip_tiles)
    return pl.pallas_call(
        kern,
        grid=(n // tn, m // tm),
        in_specs=[pl.BlockSpec((tm // 2, d), lambda j, i: (i, 0)),
                  pl.BlockSpec((None, tn, d), lambda j, i: (layer, w_tile(j), 0))],
        out_specs=pl.BlockSpec((tm, tn), lambda j, i: (i, j)),
        out_shape=jax.ShapeDtypeStruct((m, n), BF16),
        scratch_shapes=[pltpu.VMEM((tn, d), BF16)],
        compiler_params=_cparams("arbitrary", "arbitrary"),
        name="inproj",
    )(h, w_in_t)


def _gate_kernel(h_ref, wlr_ref, w2_ref, ba_ref, o_ref, *, rank):
    lr = _dot_nt(_unpack_rows(h_ref[...]), wlr_ref[...].astype(BF16))
    lr = jnp.where(lax.broadcasted_iota(jnp.int32, lr.shape, 1) < rank, lr, 0.0)
    logit = _dot(lr.astype(BF16), w2_ref[...]) + ba_ref[...]
    log_sig = jnp.minimum(logit, 0.0) - jnp.log1p(jnp.exp(-jnp.abs(logit)))
    o_ref[...] = log_sig / GLA_TAU


def _gate(h, w_in_t, w2, ba, layer, off_lr, rank):
    m, d = 2 * h.shape[0], h.shape[1]
    rp, dkt = w2.shape[1:]
    tm = _tile(m, 512)
    return pl.pallas_call(
        functools.partial(_gate_kernel, rank=rank),
        grid=(m // tm,),
        in_specs=[pl.BlockSpec((tm // 2, d), lambda i: (i, 0)),
                  pl.BlockSpec((None, rp, d), lambda i: (layer, off_lr // rp, 0)),
                  pl.BlockSpec((None, rp, dkt), lambda i: (layer, 0, 0)),
                  pl.BlockSpec((None, 1, dkt), lambda i: (layer, 0, 0))],
        out_specs=pl.BlockSpec((tm, dkt), lambda i: (i, 0)),
        out_shape=jax.ShapeDtypeStruct((m, dkt), F32),
        compiler_params=_cparams("parallel"),
        name="gate",
    )(h, w_in_t, w2, ba.reshape(ba.shape[0], 1, dkt))


def _block_row(x, period, idx):
    c, w = x.shape
    x3 = x.reshape(c // period, period, w)
    return jnp.broadcast_to(x3[:, idx:idx + 1, :], x3.shape).reshape(c, w)


def _level_factor(q, k, b2, la2, m):
    c, w = b2.shape
    pos = lax.broadcasted_iota(jnp.int32, (c, w), 0) & (2 * m - 1)
    upper = pos >= m
    if 2 * m >= 8:
        e = -jnp.abs(b2 - _block_row(b2, 2 * m, m - 1))
    elif m == 1:
        e = jnp.where(upper, la2, 0.0)
    else:
        assert m == 2
        sub = lax.broadcasted_iota(jnp.int32, (c, w), 0) & 7
        e = -jnp.abs(b2 - jnp.where(sub < 4, _block_row(b2, 8, 1), _block_row(b2, 8, 5)))
    return (jnp.where(upper, q, k) * jnp.exp2(e)).astype(BF16)


def _pair_code(n):
    row = lax.broadcasted_iota(jnp.int32, (n, n), 0)
    col = lax.broadcasted_iota(jnp.int32, (n, n), 1)
    xf = (row ^ col).astype(F32)
    high_bit = (lax.bitcast_convert_type(xf, jnp.int32) >> 23) - 127
    return jnp.where(col < row, high_bit, jnp.where(col == row, -1, -2))


def _diag_scores(q, k, b2, la2, levels, code, blocks):
    qb, kb = q.astype(BF16), k.astype(BF16)
    out = [jnp.where(code == -1, _dot_nt(qb[lo:hi], kb[lo:hi]), 0.0) for lo, hi in blocks]
    for m in levels:
        g = _level_factor(q, k, b2, la2, m)
        bit = m.bit_length() - 1
        out = [jnp.where(code == bit, _dot_nt(g[lo:hi], g[lo:hi]), a) for a, (lo, hi) in zip(out, blocks)]
    return out


def _head_norm_gate(o, gn, gg):
    return _rms(o, gn) * gg


def _levels(c):
    out, m = [], c // 2
    while m >= 1:
        out.append(m)
        m //= 2
    return tuple(out)


def _gla_prompt_kernel(q_ref, k_ref, v_ref, gg_ref, la_ref, gn_ref, zb_ref, s_ref, tril_ref, code_ref, *, chunk):
    dk = q_ref.shape[1]
    c, hc = chunk, chunk // 2

    @pl.when(pl.program_id(2) == 0)
    def _():
        s_ref[...] = jnp.zeros_like(s_ref)

    row = lax.broadcasted_iota(jnp.int32, (c, c), 0)
    col = lax.broadcasted_iota(jnp.int32, (c, c), 1)
    tril_ref[...] = (col <= row).astype(BF16)
    code_ref[...] = _pair_code(hc)

    def step(ci, carry):
        rows = pl.ds(pl.multiple_of(ci * c, c), c)
        la = la_ref[rows, :]
        tril = tril_ref[...]
        hi, mid, lo = _split3(la)
        b2 = (_dot(tril, hi) + _dot(tril, mid) + _dot(tril, lo)) * LOG2E
        la2 = la * LOG2E
        b2_last = b2[c - 1:c, :]

        q = q_ref[rows, :].astype(F32)
        k = k_ref[rows, :].astype(F32)
        v = v_ref[rows, :]
        levels = _levels(c)
        a00, a11 = _diag_scores(q, k, b2, la2, levels[1:], code_ref[...], ((0, hc), (hc, c)))
        g_top = _level_factor(q, k, b2, la2, hc)
        a10 = _dot_nt(g_top[hc:], g_top[:hc])
        s_old = s_ref[...]
        o_intra = jnp.concatenate(
            [_dot(a00.astype(BF16), v[:hc]),
             _dot(jnp.concatenate([a10, a11], axis=1).astype(BF16), v)], axis=0)
        o = o_intra + _dot((q * jnp.exp2(b2)).astype(BF16), s_old.astype(BF16))
        packed_rows = pl.ds(pl.multiple_of(ci * hc, hc), hc)
        zb_ref[packed_rows, :] = _pack_rows(_head_norm_gate(o, gn_ref[...], gg_ref[rows, :].astype(F32)))

        kd_t = (k * jnp.exp2(b2_last - b2)).T.astype(BF16)
        eye = lax.broadcasted_iota(jnp.int32, (dk, dk), 0) == lax.broadcasted_iota(jnp.int32, (dk, dk), 1)
        d_col = jnp.sum(jnp.where(eye, jnp.exp2(b2_last), 0.0), axis=1, keepdims=True)
        s_ref[...] = d_col * s_old + _dot(kd_t, v)
        return carry

    lax.fori_loop(0, q_ref.shape[0] // c, step, 0)


def _gla_prompt(p, la, gn, layer, offs, batch, seq, rows_total, heads, dk, dv, chunk, tblk):
    nb = seq // tblk
    kern = functools.partial(_gla_prompt_kernel, chunk=chunk)
    qb, kb, vb, gb = offs["q"] // dk, offs["k"] // dk, offs["v"] // dv, offs["gg"] // dv
    rows = lambda b, h, n: b * nb + n
    return pl.pallas_call(
        kern,
        grid=(batch, heads, nb),
        in_specs=[pl.BlockSpec((tblk, dk), lambda b, h, n: (rows(b, h, n), qb + h)),
                  pl.BlockSpec((tblk, dk), lambda b, h, n: (rows(b, h, n), kb + h)),
                  pl.BlockSpec((tblk, dv), lambda b, h, n: (rows(b, h, n), vb + h)),
                  pl.BlockSpec((tblk, dv), lambda b, h, n: (rows(b, h, n), gb + h)),
                  pl.BlockSpec((tblk, dk), lambda b, h, n: (rows(b, h, n), h)),
                  pl.BlockSpec((None, 1, dv), lambda b, h, n: (layer, 0, 0))],
        out_specs=[pl.BlockSpec((tblk // 2, dv), lambda b, h, n: (rows(b, h, n), h)),
                   pl.BlockSpec((None, None, dk, dv), lambda b, h, n: (b, h, 0, 0))],
        out_shape=[jax.ShapeDtypeStruct((rows_total // 2, heads * dv), jnp.uint32),
                   jax.ShapeDtypeStruct((batch, heads, dk, dv), F32)],
        scratch_shapes=[pltpu.VMEM((chunk, chunk), BF16), pltpu.VMEM((chunk // 2, chunk // 2), jnp.int32)],
        compiler_params=_cparams("parallel", "parallel", "arbitrary"),
        name="gla_prompt",
    )(p, p, p, p, la, gn.reshape(gn.shape[0], 1, dv))


def _gla_sample_body(q_ref, k_ref, v_ref, gg_ref, la_ref, gn_ref, sin_ref, zb_ref, sout_ref, o_scr):
    r, dk = q_ref.shape
    nseq = r // SAMPLE_LEN
    la = la_ref[...]
    row = lax.broadcasted_iota(jnp.int32, (r, r), 0)
    col = lax.broadcasted_iota(jnp.int32, (r, r), 1)
    tril = ((col <= row) & ((row ^ col) < SAMPLE_LEN)).astype(BF16)
    hi, mid, lo = _split3(la)
    b2 = (_dot(tril, hi) + _dot(tril, mid) + _dot(tril, lo)) * LOG2E
    b2_last = _block_row(b2, SAMPLE_LEN, SAMPLE_LEN - 1)

    q = q_ref[...].astype(F32)
    k = k_ref[...].astype(F32)
    v = v_ref[...]
    (a,) = _diag_scores(q, k, b2, la * LOG2E, _levels(SAMPLE_LEN), _pair_code(r), ((0, r),))
    o_scr[...] = _dot(a.astype(BF16), v)
    decay = jnp.exp2(b2)
    qe = (q * decay).astype(BF16)
    kd_t = (k * jnp.exp2(b2_last - b2)).T
    d_t = decay.T
    lane_seq = lax.broadcasted_iota(jnp.int32, (dk, r), 1) // SAMPLE_LEN
    for s in range(nseq):
        lo_r, hi_r = s * SAMPLE_LEN, (s + 1) * SAMPLE_LEN
        s_old = sin_ref[s]
        o_scr[lo_r:hi_r, :] += _dot(qe[lo_r:hi_r, :], s_old.astype(BF16))
        kd_s = jnp.where(lane_seq == s, kd_t, 0.0).astype(BF16)
        sout_ref[s] = d_t[:, hi_r - 1:hi_r] * s_old + _dot(kd_s, v)
    zb_ref[...] = _pack_rows(_head_norm_gate(o_scr[...], gn_ref[...], gg_ref[...].astype(F32)))


def _gla_sample_kernel(n_alias, *refs):
    _gla_sample_body(*refs[:7], *refs[7 + n_alias:])


def _gla_sample(p, la, gn, state, zb_all, new_state, layer, offs, row0, sb):
    _, nseq, heads, dk, dv = state.shape
    r = sb * SAMPLE_LEN
    rb0 = row0 // r
    qb, kb, vb, gb = offs["q"] // dk, offs["k"] // dk, offs["v"] // dv, offs["gg"] // dv
    st_spec = pl.BlockSpec((None, sb, None, dk, dv), lambda i, h: (layer, i, h, 0, 0))
    zb_spec = pl.BlockSpec((r // 2, dv), lambda i, h: (rb0 + i, h))
    in_specs = [pl.BlockSpec((r, dk), lambda i, h: (rb0 + i, qb + h)),
                pl.BlockSpec((r, dk), lambda i, h: (rb0 + i, kb + h)),
                pl.BlockSpec((r, dv), lambda i, h: (rb0 + i, vb + h)),
                pl.BlockSpec((r, dv), lambda i, h: (rb0 + i, gb + h)),
                pl.BlockSpec((r, dk), lambda i, h: (rb0 + i, h)),
                pl.BlockSpec((None, 1, dv), lambda i, h: (layer, 0, 0)),
                st_spec,
                pl.BlockSpec(memory_space=pl.ANY)]
    args = [p, p, p, p, la, gn.reshape(gn.shape[0], 1, dv), state, zb_all]
    aliases = {7: 0}
    if new_state is not None:
        in_specs.append(pl.BlockSpec(memory_space=pl.ANY))
        args.append(new_state)
        aliases[8] = 1
    return pl.pallas_call(
        functools.partial(_gla_sample_kernel, len(aliases)),
        grid=(nseq // sb, heads),
        in_specs=in_specs,
        out_specs=[zb_spec, st_spec],
        out_shape=[jax.ShapeDtypeStruct(zb_all.shape, zb_all.dtype),
                   jax.ShapeDtypeStruct(state.shape, F32)],
        scratch_shapes=[pltpu.VMEM((r, dv), F32)],
        input_output_aliases=aliases,
        compiler_params=_cparams("parallel", "parallel"),
        name="gla_sample",
    )(*args)


def _short_conv(u, um1, um2, w):
    return um2 * w[0:1, :] + um1 * w[1:2, :] + u * w[2:3, :]


def _inproj_conv_kernel(h_ref, wcb_ref, wcc_ref, wcx_ref, wcg_ref, cw_ref, st_ref, za_ref, nbp_ref, nbs_ref,
                        wb_ref, halo_ref, *, prompt_tiles, tiles_per_seq, sub_m):
    i = pl.program_id(1)
    tc = wcb_ref.shape[0]
    tm = 2 * h_ref.shape[0]

    @pl.when(i == 0)
    def _():
        for g, ref in enumerate((wcb_ref, wcc_ref, wcx_ref, wcg_ref)):
            wb_ref[g * tc:(g + 1) * tc, :] = ref[...].astype(BF16)

    w = cw_ref[...]

    def project(r):
        hh = _unpack_rows(h_ref[r * sub_m // 2:(r + 1) * sub_m // 2, :])
        bc = _dot_nt(hh, wb_ref[0:2 * tc, :])
        xg = _dot_nt(hh, wb_ref[2 * tc:4 * tc, :])
        cg = xg[:, tc:]
        return bc[:, :tc] * (cg * _sigmoid(cg)), bc[:, tc:] * xg[:, :tc]

    @pl.when(i < prompt_tiles)
    def _():
        @pl.when(i % tiles_per_seq == 0)
        def _():
            halo_ref[...] = jnp.zeros_like(halo_ref)

        for r in range(tm // sub_m):
            gate, u = project(r)
            hr = halo_ref.shape[0]
            prev = halo_ref[...]
            row = lax.broadcasted_iota(jnp.int32, u.shape, 0)
            um1 = jnp.where(row == 0, prev[hr - 1:hr, :], pltpu.roll(u, 1, 0))
            um2 = jnp.where(row == 0, prev[hr - 2:hr - 1, :],
                            jnp.where(row == 1, prev[hr - 1:hr, :], pltpu.roll(u, 2, 0)))
            za_ref[r * sub_m // 2:(r + 1) * sub_m // 2, :] = _pack_rows(gate * _short_conv(u, um1, um2, w))
            halo_ref[...] = u[sub_m - hr:sub_m, :]
        nbp_ref[...] = u[sub_m - (CONV_WIDTH - 1):sub_m, :]

    @pl.when(i >= prompt_tiles)
    def _():
        nsq = sub_m // SAMPLE_LEN
        for r in range(tm // sub_m):
            gate, u = project(r)
            st = st_ref[r * nsq:(r + 1) * nsq]
            rows = lambda x: jnp.broadcast_to(x, (nsq, SAMPLE_LEN, tc)).reshape(sub_m, tc)
            s0, s1 = rows(st[:, 0:1, :]), rows(st[:, 1:2, :])
            pos = lax.broadcasted_iota(jnp.int32, u.shape, 0) & (SAMPLE_LEN - 1)
            um1 = jnp.where(pos == 0, s1, pltpu.roll(u, 1, 0))
            um2 = jnp.where(pos == 0, s0, jnp.where(pos == 1, s1, pltpu.roll(u, 2, 0)))
            za_ref[r * sub_m // 2:(r + 1) * sub_m // 2, :] = _pack_rows(gate * _short_conv(u, um1, um2, w))
            nbs_ref[r * nsq:(r + 1) * nsq] = u.reshape(nsq, SAMPLE_LEN, tc)[:, SAMPLE_LEN - (CONV_WIDTH - 1):, :]


def _inproj_conv(h, w_in_t, conv_w, state, layer, woffs, batch, seq, tm, tc):
    m, d = 2 * h.shape[0], h.shape[1]
    nsamp, cd = state.shape[1], state.shape[3]
    pt = batch * seq // tm
    tiles_per_seq = seq // tm
    sps = tm // SAMPLE_LEN
    sub_m = min(tm, 512)
    kern = functools.partial(_inproj_conv_kernel, prompt_tiles=pt, tiles_per_seq=tiles_per_seq, sub_m=sub_m)
    w_spec = lambda name: pl.BlockSpec((None, tc, d), lambda j, i: (layer, woffs[name] // tc + j, 0))
    samp = lambda i: jnp.maximum(i - pt, 0)
    return pl.pallas_call(
        kern,
        grid=(cd // tc, m // tm),
        in_specs=[pl.BlockSpec((tm // 2, d), lambda j, i: (i, 0)),
                  w_spec("cB"), w_spec("cC"), w_spec("cx"), w_spec("cg"),
                  pl.BlockSpec((None, CONV_WIDTH, tc), lambda j, i: (layer, 0, j)),
                  pl.BlockSpec((None, sps, CONV_WIDTH - 1, tc), lambda j, i: (layer, samp(i), 0, j))],
        out_specs=[pl.BlockSpec((tm // 2, tc), lambda j, i: (i, j)),
                   pl.BlockSpec((None, CONV_WIDTH - 1, tc),
                                lambda j, i: (jnp.minimum(i // tiles_per_seq, batch - 1), 0, j)),
                   pl.BlockSpec((sps, CONV_WIDTH - 1, tc), lambda j, i: (samp(i), 0, j))],
        out_shape=[jax.ShapeDtypeStruct((m // 2, cd), jnp.uint32),
                   jax.ShapeDtypeStruct((batch, CONV_WIDTH - 1, cd), F32),
                   jax.ShapeDtypeStruct((nsamp, CONV_WIDTH - 1, cd), F32)],
        scratch_shapes=[pltpu.VMEM((4 * tc, d), BF16), pltpu.VMEM((8, tc), F32)],
        compiler_params=_cparams("arbitrary", "arbitrary"),
        name="inproj_conv",
    )(h, w_in_t, w_in_t, w_in_t, w_in_t, conv_w, state)


def _branch_kernel(za_ref, zb_ref, wa_ref, wb_ref, sga_ref, sgb_ref, o_ref, wa_bf, wb_bf):
    @pl.when(pl.program_id(1) == 0)
    def _():
        wa_bf[...] = wa_ref[...].astype(BF16)
        wb_bf[...] = wb_ref[...].astype(BF16)

    tn = o_ref.shape[1]
    sub = min(tn, MXU_WIDTH)
    for c in range(tn // sub):
        cols = slice(c * sub, (c + 1) * sub)
        a = _dot(_unpack_rows(za_ref[...]), wa_bf[:, cols])
        b = _dot(_unpack_rows(zb_ref[...]), wb_bf[:, cols])
        o_ref[:, cols] = _pack_rows(sga_ref[:, cols].astype(F32) * a + sgb_ref[:, cols].astype(F32) * b)


def _branch(za, zb, w_a, w_b, p, layer, offs, tm, tn):
    m, cd = 2 * za.shape[0], za.shape[1]
    dvt = zb.shape[1]
    d = w_a.shape[2]
    gao, gbo = offs["ga"] // tn, offs["gb"] // tn
    return pl.pallas_call(
        _branch_kernel,
        grid=(d // tn, m // tm),
        in_specs=[pl.BlockSpec((tm // 2, cd), lambda j, i: (i, 0)),
                  pl.BlockSpec((tm // 2, dvt), lambda j, i: (i, 0)),
                  pl.BlockSpec((None, cd, tn), lambda j, i: (layer, 0, j)),
                  pl.BlockSpec((None, dvt, tn), lambda j, i: (layer, 0, j)),
                  pl.BlockSpec((tm, tn), lambda j, i: (i, gao + j)),
                  pl.BlockSpec((tm, tn), lambda j, i: (i, gbo + j))],
        out_specs=pl.BlockSpec((tm // 2, tn), lambda j, i: (i, j)),
        out_shape=jax.ShapeDtypeStruct((m // 2, d), jnp.uint32),
        scratch_shapes=[pltpu.VMEM((cd, tn), BF16), pltpu.VMEM((dvt, tn), BF16)],
        compiler_params=_cparams("arbitrary", "arbitrary"),
        name="branch",
    )(za, zb, w_a, w_b, p, p)


def _out_kernel(m_ref, w_ref, x_ref, g_ref, xo_ref, ho_ref, w_bf):
    @pl.when(pl.program_id(0) == 0)
    def _():
        w_bf[...] = w_ref[...].astype(BF16)

    x = x_ref[...] + _dot(_unpack_rows(m_ref[...]), w_bf[...])
    xo_ref[...] = x
    ho_ref[...] = _pack_rows(_rms(x, g_ref[...]))


def _out_final_kernel(m_ref, w_ref, x_ref, g_ref, yp_ref, ys_ref, w_bf, *, prompt_tiles):
    i = pl.program_id(0)

    @pl.when(i == 0)
    def _():
        w_bf[...] = w_ref[...].astype(BF16)

    y = _rms(x_ref[...] + _dot(_unpack_rows(m_ref[...]), w_bf[...]), g_ref[...])

    @pl.when(i < prompt_tiles)
    def _():
        yp_ref[...] = y

    @pl.when(i >= prompt_tiles)
    def _():
        ys_ref[...] = y


def _out(merged, w_out, x, g, layer, tm, prompt_rows=None):
    m, d = x.shape
    in_specs = [pl.BlockSpec((tm // 2, d), lambda i: (i, 0)),
                pl.BlockSpec((None, d, d), lambda i: (layer, 0, 0), pipeline_mode=pl.Buffered(1)),
                pl.BlockSpec((tm, d), lambda i: (i, 0)),
                pl.BlockSpec((1, d), lambda i: (0, 0))]
    if prompt_rows is None:
        kern = _out_kernel
        out_specs = [pl.BlockSpec((tm, d), lambda i: (i, 0)), pl.BlockSpec((tm // 2, d), lambda i: (i, 0))]
        out_shape = [jax.ShapeDtypeStruct((m, d), F32), jax.ShapeDtypeStruct((m // 2, d), jnp.uint32)]
    else:
        pt = prompt_rows // tm
        kern = functools.partial(_out_final_kernel, prompt_tiles=pt)
        out_specs = [pl.BlockSpec((tm, d), lambda i: (jnp.minimum(i, pt - 1), 0)),
                     pl.BlockSpec((tm, d), lambda i: (jnp.maximum(i - pt, 0), 0))]
        out_shape = [jax.ShapeDtypeStruct((prompt_rows, d), F32), jax.ShapeDtypeStruct((m - prompt_rows, d), F32)]
    return pl.pallas_call(
        kern,
        grid=(m // tm,),
        in_specs=in_specs,
        out_specs=out_specs,
        out_shape=out_shape,
        scratch_shapes=[pltpu.VMEM((d, d), BF16)],
        compiler_params=_cparams("arbitrary"),
        name="outproj",
    )(merged, w_out, x, g.reshape(1, d))


def kernel(x_prompt, x_sample, state_conv, state_gla, norm_g, w_in, conv_w, w_alpha2, b_alpha,
           gla_norm_g, w_branch_a, w_branch_b, w_out, final_norm_g):
    batch, seq, d = x_prompt.shape
    nsamp, slen, _ = x_sample.shape
    depth, _, _, cd = state_conv.shape
    _, _, heads, dk, dv = state_gla.shape
    rank, dkt = w_alpha2.shape[1:]
    dvt = heads * dv
    assert slen == SAMPLE_LEN and dkt == heads * dk and state_conv.shape[2] == CONV_WIDTH - 1
    widths = (("ga", d), ("gb", d), ("cB", cd), ("cC", cd), ("cx", cd), ("cg", cd),
              ("q", dkt), ("k", dkt), ("v", dvt), ("gg", dvt), ("lr", rank))
    woffs, o = {}, 0
    for name, wd in widths:
        woffs[name] = o
        o += wd
    assert o == w_in.shape[2] and rank <= LANES and woffs["lr"] % LANES == 0 and cd == d
    poffs, o = {}, 0
    for name, wd in widths:
        if name in ("ga", "gb", "q", "k", "v", "gg"):
            poffs[name] = o
            o += wd
    poffs["end"] = o

    mp, ms = batch * seq, nsamp * slen
    m = mp + ms
    tm = _tile(ms, 1024)
    tn = _tile(dkt, 1024)
    tm_wide = next((k * tm for k in (3, 2) if m % (k * tm) == 0), tm)
    chunk = _tile(seq, 256)
    sb = _tile(nsamp, 16)
    w2 = jnp.pad(w_alpha2, ((0, 0), (0, LANES - rank), (0, 0))).astype(BF16)
    w_in_t = jnp.swapaxes(w_in, 1, 2)

    x, h = _embed(x_prompt.reshape(mp, d), x_sample.reshape(ms, d), norm_g[0])
    conv_p, gla_p, conv_s, gla_s = [], [], [], None
    for l in range(depth):
        p = _inproj(h, w_in_t, l, woffs, poffs, dk, tm_wide, _tile(tn, 512))
        za, nb_p, nb_s = _inproj_conv(h, w_in_t, conv_w, state_conv, l, woffs, batch, seq, tm, _tile(cd, MXU_WIDTH))
        la = _gate(h, w_in_t, w2, b_alpha, l, woffs["lr"], rank)
        zb, s_p = _gla_prompt(p, la, gla_norm_g, l, poffs, batch, seq, m, heads, dk, dv, chunk, _tile(seq, 1024))
        zb, gla_s = _gla_sample(p, la, gla_norm_g, state_gla, zb, gla_s, l, poffs, mp, sb)
        merged = _branch(za, zb, w_branch_a, w_branch_b, p, l, poffs, tm, _tile(d, 512))
        conv_p.append(nb_p)
        gla_p.append(s_p)
        conv_s.append(nb_s)
        if l < depth - 1:
            x, h = _out(merged, w_out, x, norm_g[l + 1], l, _tile(ms, 512))
        else:
            y_p, y_s = _out(merged, w_out, x, final_norm_g, l, _tile(ms, 256), prompt_rows=mp)
    return (y_p.reshape(batch, seq, d), y_s.reshape(nsamp, slen, d),
            jnp.stack(conv_p), jnp.stack(gla_p), jnp.stack(conv_s), gla_s)
```

```python
import functools

import jax
import jax.numpy as jnp
from jax import lax
from jax.experimental import pallas as pl
from jax.experimental.pallas import tpu as pltpu

F32 = jnp.float32
BF16 = jnp.bfloat16
NORM_EPS = 1e-6
GLA_TAU = 16.0
LOG2E = 1.4426950408889634
CONV_WIDTH = 3
SAMPLE_LEN = 8
LANES = 128
MXU_WIDTH = 256
VMEM_LIMIT = 56 << 20


def _cparams(*sem):
    return pltpu.CompilerParams(dimension_semantics=sem, vmem_limit_bytes=VMEM_LIMIT)


def _dot(a, b):
    return jnp.dot(a, b, preferred_element_type=F32)


def _dot_nt(a, b):
    return lax.dot_general(a, b, (((1,), (1,)), ((), ())), preferred_element_type=F32)


def _split3(x):
    hi = x.astype(BF16)
    r1 = x - hi.astype(F32)
    mid = r1.astype(BF16)
    lo = (r1 - mid.astype(F32)).astype(BF16)
    return hi, mid, lo


def _tile(n, pref):
    t = min(n, pref)
    while n % t:
        t //= 2
    return t


def _pack_rows(x):
    return pltpu.bitcast(x.astype(BF16), jnp.uint32)


def _unpack_rows(x):
    return pltpu.bitcast(x, BF16)


def _sigmoid(x):
    return 0.5 * jnp.tanh(0.5 * x) + 0.5


def _rms(x, g):
    ms = jnp.mean(x * x, axis=-1, keepdims=True)
    return x * lax.rsqrt(ms + NORM_EPS) * g


def _embed_kernel(xp_ref, xs_ref, g_ref, x_ref, h_ref, *, prompt_tiles):
    i = pl.program_id(0)

    def emit(x):
        x_ref[...] = x
        h_ref[...] = _pack_rows(_rms(x, g_ref[...]))

    @pl.when(i < prompt_tiles)
    def _():
        emit(xp_ref[...])

    @pl.when(i >= prompt_tiles)
    def _():
        emit(xs_ref[...])


def _embed(xp, xs, g):
    (mp, d), ms = xp.shape, xs.shape[0]
    tm = _tile(ms, 512)
    pt = mp // tm
    kern = functools.partial(_embed_kernel, prompt_tiles=pt)
    return pl.pallas_call(
        kern,
        grid=((mp + ms) // tm,),
        in_specs=[pl.BlockSpec((tm, d), lambda i: (jnp.minimum(i, pt - 1), 0)),
                  pl.BlockSpec((tm, d), lambda i: (jnp.maximum(i - pt, 0), 0)),
                  pl.BlockSpec((1, d), lambda i: (0, 0))],
        out_specs=[pl.BlockSpec((tm, d), lambda i: (i, 0)),
                   pl.BlockSpec((tm // 2, d), lambda i: (i, 0))],
        out_shape=[jax.ShapeDtypeStruct((mp + ms, d), F32),
                   jax.ShapeDtypeStruct(((mp + ms) // 2, d), jnp.uint32)],
        compiler_params=_cparams("arbitrary"),
        name="embed",
    )(xp, xs, g.reshape(1, d))


def _inproj_kernel(h_ref, w_ref, o_ref, wb_ref, *, sig_end, silu_lo, q_lo, q_hi, q_scale):
    j = pl.program_id(0)

    @pl.when(pl.program_id(1) == 0)
    def _():
        wb_ref[...] = w_ref[...].astype(BF16)

    is_sig = j < sig_end
    is_silu = j >= silu_lo
    is_q = (j >= q_lo) & (j < q_hi)
    gated = is_sig | is_silu
    alpha = jnp.where(gated, 0.5, jnp.where(is_q, q_scale, 1.0)).astype(F32)
    beta = jnp.where(gated, 0.5, 0.0).astype(F32)
    tm, tn = o_ref.shape
    sub_n = min(tn, 2 * MXU_WIDTH)
    sub_m = min(tm, 1024)
    for c in range(tn // sub_n):
        cols = slice(c * sub_n, (c + 1) * sub_n)
        for r in range(tm // sub_m):
            acc = _dot_nt(_unpack_rows(h_ref[r * sub_m // 2:(r + 1) * sub_m // 2, :]), wb_ref[cols, :])
            gate = alpha + beta * jnp.tanh(0.5 * acc)
            o_ref[r * sub_m:(r + 1) * sub_m, cols] = (jnp.where(is_sig, 1.0, acc) * gate).astype(o_ref.dtype)


def _inproj(h, w_in_t, layer, woffs, poffs, dk, tm, tn):
    m, d = 2 * h.shape[0], h.shape[1]
    n = poffs["end"]
    gate_tiles = poffs["q"] // tn
    skip_tiles = (woffs["q"] - woffs["cB"]) // tn
    kern = functools.partial(
        _inproj_kernel,
        sig_end=gate_tiles, silu_lo=poffs["gg"] // tn,
        q_lo=poffs["q"] // tn, q_hi=poffs["k"] // tn, q_scale=float(dk) ** -0.5)
    w_tile = lambda j: jnp.where(j < gate_tiles, j, j + skip_tiles)
    return pl.pallas_call(
        kern,
        grid=(n // tn, m // tm),
        in_specs=[pl.BlockSpec((tm // 2, d), lambda j, i: (i, 0)),
                  pl.BlockSpec((None, tn, d), lambda j, i: (layer, w_tile(j), 0))],
        out_specs=pl.BlockSpec((tm, tn), lambda j, i: (i, j)),
        out_shape=jax.ShapeDtypeStruct((m, n), BF16),
        scratch_shapes=[pltpu.VMEM((tn, d), BF16)],
        compiler_params=_cparams("arbitrary", "arbitrary"),
        name="inproj",
    )(h, w_in_t)


def _gate_kernel(h_ref, wlr_ref, w2_ref, ba_ref, o_ref, *, rank):
    lr = _dot_nt(_unpack_rows(h_ref[...]), wlr_ref[...].astype(BF16))
    lr = jnp.where(lax.broadcasted_iota(jnp.int32, lr.shape, 1) < rank, lr, 0.0)
    logit = _dot(lr.astype(BF16), w2_ref[...]) + ba_ref[...]
    log_sig = jnp.minimum(logit, 0.0) - jnp.log1p(jnp.exp(-jnp.abs(logit)))
    o_ref[...] = log_sig / GLA_TAU


def _gate(h, w_in_t, w2, ba, layer, off_lr, rank):
    m, d = 2 * h.shape[0], h.shape[1]
    rp, dkt = w2.shape[1:]
    tm = _tile(m, 512)
    return pl.pallas_call(
        functools.partial(_gate_kernel, rank=rank),
        grid=(m // tm,),
        in_specs=[pl.BlockSpec((tm // 2, d), lambda i: (i, 0)),
                  pl.BlockSpec((None, rp, d), lambda i: (layer, off_lr // rp, 0)),
                  pl.BlockSpec((None, rp, dkt), lambda i: (layer, 0, 0)),
                  pl.BlockSpec((None, 1, dkt), lambda i: (layer, 0, 0))],
        out_specs=pl.BlockSpec((tm, dkt), lambda i: (i, 0)),
        out_shape=jax.ShapeDtypeStruct((m, dkt), F32),
        compiler_params=_cparams("parallel"),
        name="gate",
    )(h, w_in_t, w2, ba.reshape(ba.shape[0], 1, dkt))


def _block_row(x, period, idx):
    c, w = x.shape
    x3 = x.reshape(c // period, period, w)
    return jnp.broadcast_to(x3[:, idx:idx + 1, :], x3.shape).reshape(c, w)


def _level_factor(q, k, b2, la2, m):
    c, w = b2.shape
    pos = lax.broadcasted_iota(jnp.int32, (c, w), 0) & (2 * m - 1)
    upper = pos >= m
    if 2 * m >= 8:
        e = -jnp.abs(b2 - _block_row(b2, 2 * m, m - 1))
    elif m == 1:
        e = jnp.where(upper, la2, 0.0)
    else:
        assert m == 2
        sub = lax.broadcasted_iota(jnp.int32, (c, w), 0) & 7
        e = -jnp.abs(b2 - jnp.where(sub < 4, _block_row(b2, 8, 1), _block_row(b2, 8, 5)))
    return (jnp.where(upper, q, k) * jnp.exp2(e)).astype(BF16)


def _pair_code(n):
    row = lax.broadcasted_iota(jnp.int32, (n, n), 0)
    col = lax.broadcasted_iota(jnp.int32, (n, n), 1)
    xf = (row ^ col).astype(F32)
    high_bit = (lax.bitcast_convert_type(xf, jnp.int32) >> 23) - 127
    return jnp.where(col < row, high_bit, jnp.where(col == row, -1, -2))


def _diag_scores(q, k, b2, la2, levels, code, blocks):
    qb, kb = q.astype(BF16), k.astype(BF16)
    out = [jnp.where(code == -1, _dot_nt(qb[lo:hi], kb[lo:hi]), 0.0) for lo, hi in blocks]
    for m in levels:
        g = _level_factor(q, k, b2, la2, m)
        bit = m.bit_length() - 1
        out = [jnp.where(code == bit, _dot_nt(g[lo:hi], g[lo:hi]), a) for a, (lo, hi) in zip(out, blocks)]
    return out


def _head_norm_gate(o, gn, gg):
    return _rms(o, gn) * gg


def _levels(c):
    out, m = [], c // 2
    while m >= 1:
        out.append(m)
        m //= 2
    return tuple(out)


def _gla_prompt_body(q_ref, k_ref, v_ref, gg_ref, la_ref, gn_ref, zb_ref, s_ref, tril_ref, code_ref, *, chunk,
                     first_block):
    dk = q_ref.shape[1]
    c, hc = chunk, chunk // 2

    @pl.when(first_block)
    def _():
        s_ref[...] = jnp.zeros_like(s_ref)

    row = lax.broadcasted_iota(jnp.int32, (c, c), 0)
    col = lax.broadcasted_iota(jnp.int32, (c, c), 1)
    tril_ref[...] = (col <= row).astype(BF16)
    code_ref[...] = _pair_code(hc)

    def step(ci, carry):
        rows = pl.ds(pl.multiple_of(ci * c, c), c)
        la = la_ref[rows, :]
        tril = tril_ref[...]
        hi, mid, lo = _split3(la)
        b2 = (_dot(tril, hi) + _dot(tril, mid) + _dot(tril, lo)) * LOG2E
        la2 = la * LOG2E
        b2_last = b2[c - 1:c, :]

        q = q_ref[rows, :].astype(F32)
        k = k_ref[rows, :].astype(F32)
        v = v_ref[rows, :]
        levels = _levels(c)
        a00, a11 = _diag_scores(q, k, b2, la2, levels[1:], code_ref[...], ((0, hc), (hc, c)))
        g_top = _level_factor(q, k, b2, la2, hc)
        a10 = _dot_nt(g_top[hc:], g_top[:hc])
        s_old = s_ref[...]
        o_intra = jnp.concatenate(
            [_dot(a00.astype(BF16), v[:hc]),
             _dot(jnp.concatenate([a10, a11], axis=1).astype(BF16), v)], axis=0)
        o = o_intra + _dot((q * jnp.exp2(b2)).astype(BF16), s_old.astype(BF16))
        packed_rows = pl.ds(pl.multiple_of(ci * hc, hc), hc)
        zb_ref[packed_rows, :] = _pack_rows(_head_norm_gate(o, gn_ref[...], gg_ref[rows, :].astype(F32)))

        kd_t = (k * jnp.exp2(b2_last - b2)).T.astype(BF16)
        eye = lax.broadcasted_iota(jnp.int32, (dk, dk), 0) == lax.broadcasted_iota(jnp.int32, (dk, dk), 1)
        d_col = jnp.sum(jnp.where(eye, jnp.exp2(b2_last), 0.0), axis=1, keepdims=True)
        s_ref[...] = d_col * s_old + _dot(kd_t, v)
        return carry

    lax.fori_loop(0, q_ref.shape[0] // c, step, 0)


def _gla_sample_body(q_ref, k_ref, v_ref, gg_ref, la_ref, gn_ref, sin_ref, zb_ref, sout_ref, o_scr):
    r, dk = q_ref.shape
    nseq = r // SAMPLE_LEN
    la = la_ref[...]
    row = lax.broadcasted_iota(jnp.int32, (r, r), 0)
    col = lax.broadcasted_iota(jnp.int32, (r, r), 1)
    tril = ((col <= row) & ((row ^ col) < SAMPLE_LEN)).astype(BF16)
    hi, mid, lo = _split3(la)
    b2 = (_dot(tril, hi) + _dot(tril, mid) + _dot(tril, lo)) * LOG2E
    b2_last = _block_row(b2, SAMPLE_LEN, SAMPLE_LEN - 1)

    q = q_ref[...].astype(F32)
    k = k_ref[...].astype(F32)
    v = v_ref[...]
    (a,) = _diag_scores(q, k, b2, la * LOG2E, _levels(SAMPLE_LEN), _pair_code(r), ((0, r),))
    o_scr[...] = _dot(a.astype(BF16), v)
    decay = jnp.exp2(b2)
    qe = (q * decay).astype(BF16)
    kd_t = (k * jnp.exp2(b2_last - b2)).T
    d_t = decay.T
    lane_seq = lax.broadcasted_iota(jnp.int32, (dk, r), 1) // SAMPLE_LEN
    for s in range(nseq):
        lo_r, hi_r = s * SAMPLE_LEN, (s + 1) * SAMPLE_LEN
        s_old = sin_ref[s]
        o_scr[lo_r:hi_r, :] += _dot(qe[lo_r:hi_r, :], s_old.astype(BF16))
        kd_s = jnp.where(lane_seq == s, kd_t, 0.0).astype(BF16)
        sout_ref[s] = d_t[:, hi_r - 1:hi_r] * s_old + _dot(kd_s, v)
    zb_ref[...] = _pack_rows(_head_norm_gate(o_scr[...], gn_ref[...], gg_ref[...].astype(F32)))


def _gla_kernel(*refs, chunk, blocks_per_seq, n_alias):
    (qp, kp, vp, ggp, lap, qs, ks, vs, ggs, las, gn, sin), rest = refs[:12], refs[12 + n_alias:]
    zbp, sp, zbs, sout, tril_ref, code_ref, o_scr = rest
    first_block = pl.program_id(0) % blocks_per_seq == 0
    _gla_prompt_body(qp, kp, vp, ggp, lap, gn, zbp, sp, tril_ref, code_ref, chunk=chunk, first_block=first_block)
    _gla_sample_body(qs, ks, vs, ggs, las, gn, sin, zbs, sout, o_scr)


def _gla(p, la, gn, state, new_state, layer, offs, batch, seq, chunk, tblk, sb):
    _, nseq, heads, dk, dv = state.shape
    nb = seq // tblk
    steps = batch * heads * nb
    assert steps == (nseq // sb) * heads
    r = sb * SAMPLE_LEN
    rb0 = batch * seq // r
    qb, kb, vb, gb = offs["q"] // dk, offs["k"] // dk, offs["v"] // dv, offs["gg"] // dv
    prow = lambda s: (s // (heads * nb)) * nb + s % nb
    phead = lambda s: (s // nb) % heads
    srow = lambda s: rb0 + s // heads
    shead = lambda s: s % heads
    st_spec = pl.BlockSpec((None, sb, None, dk, dv), lambda s: (layer, s // heads, shead(s), 0, 0))
    in_specs = [pl.BlockSpec((tblk, dk), lambda s: (prow(s), qb + phead(s))),
                pl.BlockSpec((tblk, dk), lambda s: (prow(s), kb + phead(s))),
                pl.BlockSpec((tblk, dv), lambda s: (prow(s), vb + phead(s))),
                pl.BlockSpec((tblk, dv), lambda s: (prow(s), gb + phead(s))),
                pl.BlockSpec((tblk, dk), lambda s: (prow(s), phead(s))),
                pl.BlockSpec((r, dk), lambda s: (srow(s), qb + shead(s))),
                pl.BlockSpec((r, dk), lambda s: (srow(s), kb + shead(s))),
                pl.BlockSpec((r, dv), lambda s: (srow(s), vb + shead(s))),
                pl.BlockSpec((r, dv), lambda s: (srow(s), gb + shead(s))),
                pl.BlockSpec((r, dk), lambda s: (srow(s), shead(s))),
                pl.BlockSpec((None, 1, dv), lambda s: (layer, 0, 0)),
                st_spec]
    args = [p] * 4 + [la] + [p] * 4 + [la, gn.reshape(gn.shape[0], 1, dv), state]
    aliases = {}
    if new_state is not None:
        in_specs.append(pl.BlockSpec(memory_space=pl.ANY))
        args.append(new_state)
        aliases[len(args) - 1] = 3
    return pl.pallas_call(
        functools.partial(_gla_kernel, chunk=chunk, blocks_per_seq=nb, n_alias=len(aliases)),
        grid=(steps,),
        in_specs=in_specs,
        out_specs=[pl.BlockSpec((tblk // 2, dv), lambda s: (prow(s), phead(s))),
                   pl.BlockSpec((None, None, dk, dv), lambda s: (s // (heads * nb), phead(s), 0, 0)),
                   pl.BlockSpec((r // 2, dv), lambda s: (s // heads, shead(s))),
                   st_spec],
        out_shape=[jax.ShapeDtypeStruct((batch * seq // 2, heads * dv), jnp.uint32),
                   jax.ShapeDtypeStruct((batch, heads, dk, dv), F32),
                   jax.ShapeDtypeStruct((nseq * SAMPLE_LEN // 2, heads * dv), jnp.uint32),
                   jax.ShapeDtypeStruct(state.shape, F32)],
        scratch_shapes=[pltpu.VMEM((chunk, chunk), BF16), pltpu.VMEM((chunk // 2, chunk // 2), jnp.int32),
                        pltpu.VMEM((r, dv), F32)],
        input_output_aliases=aliases,
        compiler_params=_cparams("arbitrary"),
        name="gla",
    )(*args)


def _short_conv(u, um1, um2, w):
    return um2 * w[0:1, :] + um1 * w[1:2, :] + u * w[2:3, :]


def _inproj_conv_kernel(h_ref, wcb_ref, wcc_ref, wcx_ref, wcg_ref, cw_ref, st_ref, za_ref, nbp_ref, nbs_ref,
                        wb_ref, halo_ref, *, prompt_tiles, tiles_per_seq, sub_m):
    i = pl.program_id(1)
    tc = wcb_ref.shape[0]
    tm = 2 * h_ref.shape[0]

    @pl.when(i == 0)
    def _():
        for g, ref in enumerate((wcb_ref, wcc_ref, wcx_ref, wcg_ref)):
            wb_ref[g * tc:(g + 1) * tc, :] = ref[...].astype(BF16)

    w = cw_ref[...]

    def project(r):
        hh = _unpack_rows(h_ref[r * sub_m // 2:(r + 1) * sub_m // 2, :])
        bc = _dot_nt(hh, wb_ref[0:2 * tc, :])
        xg = _dot_nt(hh, wb_ref[2 * tc:4 * tc, :])
        cg = xg[:, tc:]
        return bc[:, :tc] * (cg * _sigmoid(cg)), bc[:, tc:] * xg[:, :tc]

    @pl.when(i < prompt_tiles)
    def _():
        @pl.when(i % tiles_per_seq == 0)
        def _():
            halo_ref[...] = jnp.zeros_like(halo_ref)

        for r in range(tm // sub_m):
            gate, u = project(r)
            hr = halo_ref.shape[0]
            prev = halo_ref[...]
            row = lax.broadcasted_iota(jnp.int32, u.shape, 0)
            um1 = jnp.where(row == 0, prev[hr - 1:hr, :], pltpu.roll(u, 1, 0))
            um2 = jnp.where(row == 0, prev[hr - 2:hr - 1, :],
                            jnp.where(row == 1, prev[hr - 1:hr, :], pltpu.roll(u, 2, 0)))
            za_ref[r * sub_m // 2:(r + 1) * sub_m // 2, :] = _pack_rows(gate * _short_conv(u, um1, um2, w))
            halo_ref[...] = u[sub_m - hr:sub_m, :]
        nbp_ref[...] = u[sub_m - (CONV_WIDTH - 1):sub_m, :]

    @pl.when(i >= prompt_tiles)
    def _():
        nsq = sub_m // SAMPLE_LEN
        for r in range(tm // sub_m):
            gate, u = project(r)
            st = st_ref[r * nsq:(r + 1) * nsq]
            rows = lambda x: jnp.broadcast_to(x, (nsq, SAMPLE_LEN, tc)).reshape(sub_m, tc)
            s0, s1 = rows(st[:, 0:1, :]), rows(st[:, 1:2, :])
            pos = lax.broadcasted_iota(jnp.int32, u.shape, 0) & (SAMPLE_LEN - 1)
            um1 = jnp.where(pos == 0, s1, pltpu.roll(u, 1, 0))
            um2 = jnp.where(pos == 0, s0, jnp.where(pos == 1, s1, pltpu.roll(u, 2, 0)))
            za_ref[r * sub_m // 2:(r + 1) * sub_m // 2, :] = _pack_rows(gate * _short_conv(u, um1, um2, w))
            nbs_ref[r * nsq:(r + 1) * nsq] = u.reshape(nsq, SAMPLE_LEN, tc)[:, SAMPLE_LEN - (CONV_WIDTH - 1):, :]


def _inproj_conv(h, w_in_t, conv_w, state, layer, woffs, batch, seq, tm, tc):
    m, d = 2 * h.shape[0], h.shape[1]
    nsamp, cd = state.shape[1], state.shape[3]
    pt = batch * seq // tm
    tiles_per_seq = seq // tm
    sps = tm // SAMPLE_LEN
    sub_m = min(tm, 512)
    kern = functools.partial(_inproj_conv_kernel, prompt_tiles=pt, tiles_per_seq=tiles_per_seq, sub_m=sub_m)
    w_spec = lambda name: pl.BlockSpec((None, tc, d), lambda j, i: (layer, woffs[name] // tc + j, 0))
    samp = lambda i: jnp.maximum(i - pt, 0)
    return pl.pallas_call(
        kern,
        grid=(cd // tc, m // tm),
        in_specs=[pl.BlockSpec((tm // 2, d), lambda j, i: (i, 0)),
                  w_spec("cB"), w_spec("cC"), w_spec("cx"), w_spec("cg"),
                  pl.BlockSpec((None, CONV_WIDTH, tc), lambda j, i: (layer, 0, j)),
                  pl.BlockSpec((None, sps, CONV_WIDTH - 1, tc), lambda j, i: (layer, samp(i), 0, j))],
        out_specs=[pl.BlockSpec((tm // 2, tc), lambda j, i: (i, j)),
                   pl.BlockSpec((None, CONV_WIDTH - 1, tc),
                                lambda j, i: (jnp.minimum(i // tiles_per_seq, batch - 1), 0, j)),
                   pl.BlockSpec((sps, CONV_WIDTH - 1, tc), lambda j, i: (samp(i), 0, j))],
        out_shape=[jax.ShapeDtypeStruct((m // 2, cd), jnp.uint32),
                   jax.ShapeDtypeStruct((batch, CONV_WIDTH - 1, cd), F32),
                   jax.ShapeDtypeStruct((nsamp, CONV_WIDTH - 1, cd), F32)],
        scratch_shapes=[pltpu.VMEM((4 * tc, d), BF16), pltpu.VMEM((8, tc), F32)],
        compiler_params=_cparams("arbitrary", "arbitrary"),
        name="inproj_conv",
    )(h, w_in_t, w_in_t, w_in_t, w_in_t, conv_w, state)


def _branch_kernel(za_ref, zbp_ref, zbs_ref, wa_ref, wb_ref, sga_ref, sgb_ref, o_ref, wa_bf, wb_bf, *, prompt_tiles):
    i = pl.program_id(1)

    @pl.when(i == 0)
    def _():
        wa_bf[...] = wa_ref[...].astype(BF16)
        wb_bf[...] = wb_ref[...].astype(BF16)

    def merge(zb_ref):
        tn = o_ref.shape[1]
        sub = min(tn, MXU_WIDTH)
        for c in range(tn // sub):
            cols = slice(c * sub, (c + 1) * sub)
            a = _dot(_unpack_rows(za_ref[...]), wa_bf[:, cols])
            b = _dot(_unpack_rows(zb_ref[...]), wb_bf[:, cols])
            o_ref[:, cols] = _pack_rows(sga_ref[:, cols].astype(F32) * a + sgb_ref[:, cols].astype(F32) * b)

    @pl.when(i < prompt_tiles)
    def _():
        merge(zbp_ref)

    @pl.when(i >= prompt_tiles)
    def _():
        merge(zbs_ref)


def _branch(za, zb_p, zb_s, w_a, w_b, p, layer, offs, tm, tn):
    m, cd = 2 * za.shape[0], za.shape[1]
    dvt = zb_p.shape[1]
    d = w_a.shape[2]
    pt = 2 * zb_p.shape[0] // tm
    gao, gbo = offs["ga"] // tn, offs["gb"] // tn
    return pl.pallas_call(
        functools.partial(_branch_kernel, prompt_tiles=pt),
        grid=(d // tn, m // tm),
        in_specs=[pl.BlockSpec((tm // 2, cd), lambda j, i: (i, 0)),
                  pl.BlockSpec((tm // 2, dvt), lambda j, i: (jnp.minimum(i, pt - 1), 0)),
                  pl.BlockSpec((tm // 2, dvt), lambda j, i: (jnp.maximum(i - pt, 0), 0)),
                  pl.BlockSpec((None, cd, tn), lambda j, i: (layer, 0, j)),
                  pl.BlockSpec((None, dvt, tn), lambda j, i: (layer, 0, j)),
                  pl.BlockSpec((tm, tn), lambda j, i: (i, gao + j)),
                  pl.BlockSpec((tm, tn), lambda j, i: (i, gbo + j))],
        out_specs=pl.BlockSpec((tm // 2, tn), lambda j, i: (i, j)),
        out_shape=jax.ShapeDtypeStruct((m // 2, d), jnp.uint32),
        scratch_shapes=[pltpu.VMEM((cd, tn), BF16), pltpu.VMEM((dvt, tn), BF16)],
        compiler_params=_cparams("arbitrary", "arbitrary"),
        name="branch",
    )(za, zb_p, zb_s, w_a, w_b, p, p)


def _out_kernel(m_ref, w_ref, x_ref, g_ref, xo_ref, ho_ref, w_bf):
    @pl.when(pl.program_id(0) == 0)
    def _():
        w_bf[...] = w_ref[...].astype(BF16)

    x = x_ref[...] + _dot(_unpack_rows(m_ref[...]), w_bf[...])
    xo_ref[...] = x
    ho_ref[...] = _pack_rows(_rms(x, g_ref[...]))


def _out_final_kernel(m_ref, w_ref, x_ref, g_ref, yp_ref, ys_ref, w_bf, *, prompt_tiles):
    i = pl.program_id(0)

    @pl.when(i == 0)
    def _():
        w_bf[...] = w_ref[...].astype(BF16)

    y = _rms(x_ref[...] + _dot(_unpack_rows(m_ref[...]), w_bf[...]), g_ref[...])

    @pl.when(i < prompt_tiles)
    def _():
        yp_ref[...] = y

    @pl.when(i >= prompt_tiles)
    def _():
        ys_ref[...] = y


def _out(merged, w_out, x, g, layer, tm, prompt_rows=None):
    m, d = x.shape
    in_specs = [pl.BlockSpec((tm // 2, d), lambda i: (i, 0)),
                pl.BlockSpec((None, d, d), lambda i: (layer, 0, 0), pipeline_mode=pl.Buffered(1)),
                pl.BlockSpec((tm, d), lambda i: (i, 0)),
                pl.BlockSpec((1, d), lambda i: (0, 0))]
    if prompt_rows is None:
        kern = _out_kernel
        out_specs = [pl.BlockSpec((tm, d), lambda i: (i, 0)), pl.BlockSpec((tm // 2, d), lambda i: (i, 0))]
        out_shape = [jax.ShapeDtypeStruct((m, d), F32), jax.ShapeDtypeStruct((m // 2, d), jnp.uint32)]
    else:
        pt = prompt_rows // tm
        kern = functools.partial(_out_final_kernel, prompt_tiles=pt)
        out_specs = [pl.BlockSpec((tm, d), lambda i: (jnp.minimum(i, pt - 1), 0)),
                     pl.BlockSpec((tm, d), lambda i: (jnp.maximum(i - pt, 0), 0))]
        out_shape = [jax.ShapeDtypeStruct((prompt_rows, d), F32), jax.ShapeDtypeStruct((m - prompt_rows, d), F32)]
    return pl.pallas_call(
        kern,
        grid=(m // tm,),
        in_specs=in_specs,
        out_specs=out_specs,
        out_shape=out_shape,
        scratch_shapes=[pltpu.VMEM((d, d), BF16)],
        compiler_params=_cparams("arbitrary"),
        name="outproj",
    )(merged, w_out, x, g.reshape(1, d))


def kernel(x_prompt, x_sample, state_conv, state_gla, norm_g, w_in, conv_w, w_alpha2, b_alpha,
           gla_norm_g, w_branch_a, w_branch_b, w_out, final_norm_g):
    batch, seq, d = x_prompt.shape
    nsamp, slen, _ = x_sample.shape
    depth, _, _, cd = state_conv.shape
    _, _, heads, dk, dv = state_gla.shape
    rank, dkt = w_alpha2.shape[1:]
    dvt = heads * dv
    assert slen == SAMPLE_LEN and dkt == heads * dk and state_conv.shape[2] == CONV_WIDTH - 1
    widths = (("ga", d), ("gb", d), ("cB", cd), ("cC", cd), ("cx", cd), ("cg", cd),
              ("q", dkt), ("k", dkt), ("v", dvt), ("gg", dvt), ("lr", rank))
    woffs, o = {}, 0
    for name, wd in widths:
        woffs[name] = o
        o += wd
    assert o == w_in.shape[2] and rank <= LANES and woffs["lr"] % LANES == 0 and cd == d
    poffs, o = {}, 0
    for name, wd in widths:
        if name in ("ga", "gb", "q", "k", "v", "gg"):
            poffs[name] = o
            o += wd
    poffs["end"] = o

    mp, ms = batch * seq, nsamp * slen
    m = mp + ms
    tm = _tile(ms, 1024)
    tn = _tile(dkt, 1024)
    tm_wide = next((k * tm for k in (3, 2) if m % (k * tm) == 0), tm)
    chunk = _tile(seq, 256)
    sb = _tile(nsamp, 16)
    tblk = seq * batch * sb // nsamp
    assert seq % tblk == 0 and tblk % chunk == 0
    w2 = jnp.pad(w_alpha2, ((0, 0), (0, LANES - rank), (0, 0))).astype(BF16)
    w_in_t = jnp.swapaxes(w_in, 1, 2)

    x, h = _embed(x_prompt.reshape(mp, d), x_sample.reshape(ms, d), norm_g[0])
    conv_p, gla_p, conv_s, gla_s = [], [], [], None
    for l in range(depth):
        p = _inproj(h, w_in_t, l, woffs, poffs, dk, tm_wide, _tile(tn, 512))
        za, nb_p, nb_s = _inproj_conv(h, w_in_t, conv_w, state_conv, l, woffs, batch, seq, tm, _tile(cd, MXU_WIDTH))
        la = _gate(h, w_in_t, w2, b_alpha, l, woffs["lr"], rank)
        zb_p, s_p, zb_s, gla_s = _gla(p, la, gla_norm_g, state_gla, gla_s, l, poffs, batch, seq, chunk, tblk, sb)
        merged = _branch(za, zb_p, zb_s, w_branch_a, w_branch_b, p, l, poffs, tm, _tile(d, 512))
        conv_p.append(nb_p)
        gla_p.append(s_p)
        conv_s.append(nb_s)
        if l < depth - 1:
            x, h = _out(merged, w_out, x, norm_g[l + 1], l, _tile(ms, 512))
        else:
            y_p, y_s = _out(merged, w_out, x, final_norm_g, l, _tile(ms, 256), prompt_rows=mp)
    return (y_p.reshape(batch, seq, d), y_s.reshape(nsamp, slen, d),
            jnp.stack(conv_p), jnp.stack(gla_p), jnp.stack(conv_s), gla_s)
```

```python
import functools

import jax
import jax.numpy as jnp
from jax import lax
from jax.experimental import pallas as pl
from jax.experimental.pallas import tpu as pltpu

F32 = jnp.float32
BF16 = jnp.bfloat16
NORM_EPS = 1e-6
GLA_TAU = 16.0
LOG2E = 1.4426950408889634
CONV_WIDTH = 3
SAMPLE_LEN = 8
LANES = 128
MXU_WIDTH = 256
VMEM_LIMIT = 56 << 20


def _cparams(*sem):
    return pltpu.CompilerParams(dimension_semantics=sem, vmem_limit_bytes=VMEM_LIMIT)


def _dot(a, b):
    return jnp.dot(a, b, preferred_element_type=F32)


def _dot_nt(a, b):
    return lax.dot_general(a, b, (((1,), (1,)), ((), ())), preferred_element_type=F32)


def _split3(x):
    hi = x.astype(BF16)
    r1 = x - hi.astype(F32)
    mid = r1.astype(BF16)
    lo = (r1 - mid.astype(F32)).astype(BF16)
    return hi, mid, lo


def _tile(n, pref):
    t = min(n, pref)
    while n % t:
        t //= 2
    return t


def _pack_rows(x):
    return pltpu.bitcast(x.astype(BF16), jnp.uint32)


def _unpack_rows(x):
    return pltpu.bitcast(x, BF16)


def _sigmoid(x):
    return 0.5 * jnp.tanh(0.5 * x) + 0.5


def _rms(x, g):
    ms = jnp.mean(x * x, axis=-1, keepdims=True)
    return x * lax.rsqrt(ms + NORM_EPS) * g


def _embed_kernel(xp_ref, xs_ref, g_ref, x_ref, h_ref, *, prompt_tiles):
    i = pl.program_id(0)

    def emit(x):
        x_ref[...] = x
        h_ref[...] = _pack_rows(_rms(x, g_ref[...]))

    @pl.when(i < prompt_tiles)
    def _():
        emit(xp_ref[...])

    @pl.when(i >= prompt_tiles)
    def _():
        emit(xs_ref[...])


def _embed(xp, xs, g):
    (mp, d), ms = xp.shape, xs.shape[0]
    tm = _tile(ms, 512)
    pt = mp // tm
    kern = functools.partial(_embed_kernel, prompt_tiles=pt)
    return pl.pallas_call(
        kern,
        grid=((mp + ms) // tm,),
        in_specs=[pl.BlockSpec((tm, d), lambda i: (jnp.minimum(i, pt - 1), 0)),
                  pl.BlockSpec((tm, d), lambda i: (jnp.maximum(i - pt, 0), 0)),
                  pl.BlockSpec((1, d), lambda i: (0, 0))],
        out_specs=[pl.BlockSpec((tm, d), lambda i: (i, 0)),
                   pl.BlockSpec((tm // 2, d), lambda i: (i, 0))],
        out_shape=[jax.ShapeDtypeStruct((mp + ms, d), F32),
                   jax.ShapeDtypeStruct(((mp + ms) // 2, d), jnp.uint32)],
        compiler_params=_cparams("arbitrary"),
        name="embed",
    )(xp, xs, g.reshape(1, d))


def _inproj_kernel(h_ref, w_ref, o_ref, wb_ref, *, sig_end, silu_lo, q_lo, q_hi, q_scale):
    j = pl.program_id(0)

    @pl.when(pl.program_id(1) == 0)
    def _():
        wb_ref[...] = w_ref[...].T.astype(BF16)

    is_sig = j < sig_end
    is_silu = j >= silu_lo
    is_q = (j >= q_lo) & (j < q_hi)
    gated = is_sig | is_silu
    alpha = jnp.where(gated, 0.5, jnp.where(is_q, q_scale, 1.0)).astype(F32)
    beta = jnp.where(gated, 0.5, 0.0).astype(F32)
    tm, tn = o_ref.shape
    sub_n = min(tn, 2 * MXU_WIDTH)
    sub_m = next((s for s in (768, 1024, 512) if tm % s == 0), tm)
    for c in range(tn // sub_n):
        cols = slice(c * sub_n, (c + 1) * sub_n)
        for r in range(tm // sub_m):
            acc = _dot(_unpack_rows(h_ref[r * sub_m // 2:(r + 1) * sub_m // 2, :]), wb_ref[:, cols])
            gate = alpha + beta * jnp.tanh(0.5 * acc)
            o_ref[r * sub_m:(r + 1) * sub_m, cols] = (jnp.where(is_sig, 1.0, acc) * gate).astype(o_ref.dtype)


def _inproj(h, w_in_t, layer, woffs, poffs, dk, tm, tn):
    m, d = 2 * h.shape[0], h.shape[1]
    n = poffs["end"]
    gate_tiles = poffs["q"] // tn
    skip_tiles = (woffs["q"] - woffs["cB"]) // tn
    kern = functools.partial(
        _inproj_kernel,
        sig_end=gate_tiles, silu_lo=poffs["gg"] // tn,
        q_lo=poffs["q"] // tn, q_hi=poffs["k"] // tn, q_scale=float(dk) ** -0.5)
    w_tile = lambda j: jnp.where(j < gate_tiles, j, j + skip_tiles)
    return pl.pallas_call(
        kern,
        grid=(n // tn, m // tm),
        in_specs=[pl.BlockSpec((tm // 2, d), lambda j, i: (i, 0)),
                  pl.BlockSpec((None, tn, d), lambda j, i: (layer, w_tile(j), 0))],
        out_specs=pl.BlockSpec((tm, tn), lambda j, i: (i, j)),
        out_shape=jax.ShapeDtypeStruct((m, n), BF16),
        scratch_shapes=[pltpu.VMEM((d, tn), BF16)],
        compiler_params=_cparams("arbitrary", "arbitrary"),
        name="inproj",
    )(h, w_in_t)


def _gate_kernel(h_ref, wlr_ref, w2_ref, ba_ref, o_ref, *, rank):
    lr = _dot_nt(_unpack_rows(h_ref[...]), wlr_ref[...].astype(BF16))
    lr = jnp.where(lax.broadcasted_iota(jnp.int32, lr.shape, 1) < rank, lr, 0.0)
    logit = _dot(lr.astype(BF16), w2_ref[...]) + ba_ref[...]
    log_sig = jnp.minimum(logit, 0.0) - jnp.log1p(jnp.exp(-jnp.abs(logit)))
    o_ref[...] = log_sig / GLA_TAU


def _gate(h, w_in_t, w2, ba, layer, off_lr, rank):
    m, d = 2 * h.shape[0], h.shape[1]
    rp, dkt = w2.shape[1:]
    tm = _tile(m, 512)
    return pl.pallas_call(
        functools.partial(_gate_kernel, rank=rank),
        grid=(m // tm,),
        in_specs=[pl.BlockSpec((tm // 2, d), lambda i: (i, 0)),
                  pl.BlockSpec((None, rp, d), lambda i: (layer, off_lr // rp, 0)),
                  pl.BlockSpec((None, rp, dkt), lambda i: (layer, 0, 0)),
                  pl.BlockSpec((None, 1, dkt), lambda i: (layer, 0, 0))],
        out_specs=pl.BlockSpec((tm, dkt), lambda i: (i, 0)),
        out_shape=jax.ShapeDtypeStruct((m, dkt), F32),
        compiler_params=_cparams("parallel"),
        name="gate",
    )(h, w_in_t, w2, ba.reshape(ba.shape[0], 1, dkt))


def _block_row(x, period, idx):
    c, w = x.shape
    x3 = x.reshape(c // period, period, w)
    return jnp.broadcast_to(x3[:, idx:idx + 1, :], x3.shape).reshape(c, w)


def _level_factor(q, k, b2, la2, m):
    c, w = b2.shape
    pos = lax.broadcasted_iota(jnp.int32, (c, w), 0) & (2 * m - 1)
    upper = pos >= m
    if 2 * m >= 8:
        e = -jnp.abs(b2 - _block_row(b2, 2 * m, m - 1))
    elif m == 1:
        e = jnp.where(upper, la2, 0.0)
    else:
        assert m == 2
        sub = lax.broadcasted_iota(jnp.int32, (c, w), 0) & 7
        e = -jnp.abs(b2 - jnp.where(sub < 4, _block_row(b2, 8, 1), _block_row(b2, 8, 5)))
    return (jnp.where(upper, q, k) * jnp.exp2(e)).astype(BF16)


def _pair_code(n):
    row = lax.broadcasted_iota(jnp.int32, (n, n), 0)
    col = lax.broadcasted_iota(jnp.int32, (n, n), 1)
    xf = (row ^ col).astype(F32)
    high_bit = (lax.bitcast_convert_type(xf, jnp.int32) >> 23) - 127
    return jnp.where(col < row, high_bit, jnp.where(col == row, -1, -2))


def _diag_scores(q, k, b2, la2, levels, code, blocks):
    qb, kb = q.astype(BF16), k.astype(BF16)
    out = [jnp.where(code == -1, _dot_nt(qb[lo:hi], kb[lo:hi]), 0.0) for lo, hi in blocks]
    for m in levels:
        g = _level_factor(q, k, b2, la2, m)
        bit = m.bit_length() - 1
        out = [jnp.where(code == bit, _dot_nt(g[lo:hi], g[lo:hi]), a) for a, (lo, hi) in zip(out, blocks)]
    return out


def _head_norm_gate(o, gn, gg):
    return _rms(o, gn) * gg


def _levels(c):
    out, m = [], c // 2
    while m >= 1:
        out.append(m)
        m //= 2
    return tuple(out)


def _gla_prompt_body(q_ref, k_ref, v_ref, gg_ref, la_ref, gn_ref, zb_ref, s_ref, tril_ref, code_ref, *, chunk,
                     first_block):
    dk = q_ref.shape[1]
    c, hc = chunk, chunk // 2

    @pl.when(first_block)
    def _():
        s_ref[...] = jnp.zeros_like(s_ref)

    row = lax.broadcasted_iota(jnp.int32, (c, c), 0)
    col = lax.broadcasted_iota(jnp.int32, (c, c), 1)
    tril_ref[...] = (col <= row).astype(BF16)
    code_ref[...] = _pair_code(hc)

    def step(ci, carry):
        rows = pl.ds(pl.multiple_of(ci * c, c), c)
        la = la_ref[rows, :]
        tril = tril_ref[...]
        hi, mid, lo = _split3(la)
        b2 = (_dot(tril, hi) + _dot(tril, mid) + _dot(tril, lo)) * LOG2E
        la2 = la * LOG2E
        b2_last = b2[c - 1:c, :]

        q = q_ref[rows, :].astype(F32)
        k = k_ref[rows, :].astype(F32)
        v = v_ref[rows, :]
        levels = _levels(c)
        a00, a11 = _diag_scores(q, k, b2, la2, levels[1:], code_ref[...], ((0, hc), (hc, c)))
        g_top = _level_factor(q, k, b2, la2, hc)
        a10 = _dot_nt(g_top[hc:], g_top[:hc])
        s_old = s_ref[...]
        o_intra = jnp.concatenate(
            [_dot(a00.astype(BF16), v[:hc]),
             _dot(jnp.concatenate([a10, a11], axis=1).astype(BF16), v)], axis=0)
        o = o_intra + _dot((q * jnp.exp2(b2)).astype(BF16), s_old.astype(BF16))
        packed_rows = pl.ds(pl.multiple_of(ci * hc, hc), hc)
        zb_ref[packed_rows, :] = _pack_rows(_head_norm_gate(o, gn_ref[...], gg_ref[rows, :].astype(F32)))

        kd_t = (k * jnp.exp2(b2_last - b2)).T.astype(BF16)
        eye = lax.broadcasted_iota(jnp.int32, (dk, dk), 0) == lax.broadcasted_iota(jnp.int32, (dk, dk), 1)
        d_col = jnp.sum(jnp.where(eye, jnp.exp2(b2_last), 0.0), axis=1, keepdims=True)
        s_ref[...] = d_col * s_old + _dot(kd_t, v)
        return carry

    lax.fori_loop(0, q_ref.shape[0] // c, step, 0)


def _gla_sample_body(q_ref, k_ref, v_ref, gg_ref, la_ref, gn_ref, sin_ref, zb_ref, sout_ref, o_scr):
    r, dk = q_ref.shape
    nseq = r // SAMPLE_LEN
    la = la_ref[...]
    row = lax.broadcasted_iota(jnp.int32, (r, r), 0)
    col = lax.broadcasted_iota(jnp.int32, (r, r), 1)
    tril = ((col <= row) & ((row ^ col) < SAMPLE_LEN)).astype(BF16)
    hi, mid, lo = _split3(la)
    b2 = (_dot(tril, hi) + _dot(tril, mid) + _dot(tril, lo)) * LOG2E
    b2_last = _block_row(b2, SAMPLE_LEN, SAMPLE_LEN - 1)

    q = q_ref[...].astype(F32)
    k = k_ref[...].astype(F32)
    v = v_ref[...]
    (a,) = _diag_scores(q, k, b2, la * LOG2E, _levels(SAMPLE_LEN), _pair_code(r), ((0, r),))
    o_scr[...] = _dot(a.astype(BF16), v)
    decay = jnp.exp2(b2)
    qe = (q * decay).astype(BF16)
    kd_t = (k * jnp.exp2(b2_last - b2)).T
    d_t = decay.T
    lane_seq = lax.broadcasted_iota(jnp.int32, (dk, r), 1) // SAMPLE_LEN
    for s in range(nseq):
        lo_r, hi_r = s * SAMPLE_LEN, (s + 1) * SAMPLE_LEN
        s_old = sin_ref[s]
        o_scr[lo_r:hi_r, :] += _dot(qe[lo_r:hi_r, :], s_old.astype(BF16))
        kd_s = jnp.where(lane_seq == s, kd_t, 0.0).astype(BF16)
        sout_ref[s] = d_t[:, hi_r - 1:hi_r] * s_old + _dot(kd_s, v)
    zb_ref[...] = _pack_rows(_head_norm_gate(o_scr[...], gn_ref[...], gg_ref[...].astype(F32)))


def _gla_kernel(*refs, chunk, blocks_per_seq, n_alias):
    (qp, kp, vp, ggp, lap, qs, ks, vs, ggs, las, gn, sin), rest = refs[:12], refs[12 + n_alias:]
    zbp, sp, zbs, sout, tril_ref, code_ref, o_scr = rest
    first_block = pl.program_id(0) % blocks_per_seq == 0
    _gla_prompt_body(qp, kp, vp, ggp, lap, gn, zbp, sp, tril_ref, code_ref, chunk=chunk, first_block=first_block)
    _gla_sample_body(qs, ks, vs, ggs, las, gn, sin, zbs, sout, o_scr)


def _gla(p, la, gn, state, new_state, layer, offs, batch, seq, chunk, tblk, sb):
    _, nseq, heads, dk, dv = state.shape
    nb = seq // tblk
    steps = batch * heads * nb
    assert steps == (nseq // sb) * heads
    r = sb * SAMPLE_LEN
    rb0 = batch * seq // r
    qb, kb, vb, gb = offs["q"] // dk, offs["k"] // dk, offs["v"] // dv, offs["gg"] // dv
    prow = lambda s: (s // (heads * nb)) * nb + s % nb
    phead = lambda s: (s // nb) % heads
    srow = lambda s: rb0 + s // heads
    shead = lambda s: s % heads
    st_spec = pl.BlockSpec((None, sb, None, dk, dv), lambda s: (layer, s // heads, shead(s), 0, 0))
    in_specs = [pl.BlockSpec((tblk, dk), lambda s: (prow(s), qb + phead(s))),
                pl.BlockSpec((tblk, dk), lambda s: (prow(s), kb + phead(s))),
                pl.BlockSpec((tblk, dv), lambda s: (prow(s), vb + phead(s))),
                pl.BlockSpec((tblk, dv), lambda s: (prow(s), gb + phead(s))),
                pl.BlockSpec((tblk, dk), lambda s: (prow(s), phead(s))),
                pl.BlockSpec((r, dk), lambda s: (srow(s), qb + shead(s))),
                pl.BlockSpec((r, dk), lambda s: (srow(s), kb + shead(s))),
                pl.BlockSpec((r, dv), lambda s: (srow(s), vb + shead(s))),
                pl.BlockSpec((r, dv), lambda s: (srow(s), gb + shead(s))),
                pl.BlockSpec((r, dk), lambda s: (srow(s), shead(s))),
                pl.BlockSpec((None, 1, dv), lambda s: (layer, 0, 0)),
                st_spec]
    args = [p] * 4 + [la] + [p] * 4 + [la, gn.reshape(gn.shape[0], 1, dv), state]
    aliases = {}
    if new_state is not None:
        in_specs.append(pl.BlockSpec(memory_space=pl.ANY))
        args.append(new_state)
        aliases[len(args) - 1] = 3
    return pl.pallas_call(
        functools.partial(_gla_kernel, chunk=chunk, blocks_per_seq=nb, n_alias=len(aliases)),
        grid=(steps,),
        in_specs=in_specs,
        out_specs=[pl.BlockSpec((tblk // 2, dv), lambda s: (prow(s), phead(s))),
                   pl.BlockSpec((None, None, dk, dv), lambda s: (s // (heads * nb), phead(s), 0, 0)),
                   pl.BlockSpec((r // 2, dv), lambda s: (s // heads, shead(s))),
                   st_spec],
        out_shape=[jax.ShapeDtypeStruct((batch * seq // 2, heads * dv), jnp.uint32),
                   jax.ShapeDtypeStruct((batch, heads, dk, dv), F32),
                   jax.ShapeDtypeStruct((nseq * SAMPLE_LEN // 2, heads * dv), jnp.uint32),
                   jax.ShapeDtypeStruct(state.shape, F32)],
        scratch_shapes=[pltpu.VMEM((chunk, chunk), BF16), pltpu.VMEM((chunk // 2, chunk // 2), jnp.int32),
                        pltpu.VMEM((r, dv), F32)],
        input_output_aliases=aliases,
        compiler_params=_cparams("arbitrary"),
        name="gla",
    )(*args)


def _short_conv(u, um1, um2, w):
    return um2 * w[0:1, :] + um1 * w[1:2, :] + u * w[2:3, :]


def _inproj_conv_kernel(h_ref, wcb_ref, wcc_ref, wcx_ref, wcg_ref, cw_ref, st_ref, za_ref, nbp_ref, nbs_ref,
                        wb_ref, halo_ref, *, prompt_tiles, tiles_per_seq, sub_m):
    i = pl.program_id(1)
    tc = wcb_ref.shape[0]
    tm = 2 * h_ref.shape[0]

    @pl.when(i == 0)
    def _():
        for g, ref in enumerate((wcb_ref, wcc_ref, wcx_ref, wcg_ref)):
            wb_ref[:, g * tc:(g + 1) * tc] = ref[...].T.astype(BF16)

    w = cw_ref[...]

    def project(r):
        hh = _unpack_rows(h_ref[r * sub_m // 2:(r + 1) * sub_m // 2, :])
        bc = _dot(hh, wb_ref[:, 0:2 * tc])
        xg = _dot(hh, wb_ref[:, 2 * tc:4 * tc])
        cg = xg[:, tc:]
        return bc[:, :tc] * (cg * _sigmoid(cg)), bc[:, tc:] * xg[:, :tc]

    @pl.when(i < prompt_tiles)
    def _():
        @pl.when(i % tiles_per_seq == 0)
        def _():
            halo_ref[...] = jnp.zeros_like(halo_ref)

        for r in range(tm // sub_m):
            gate, u = project(r)
            hr = halo_ref.shape[0]
            prev = halo_ref[...]
            row = lax.broadcasted_iota(jnp.int32, u.shape, 0)
            um1 = jnp.where(row == 0, prev[hr - 1:hr, :], pltpu.roll(u, 1, 0))
            um2 = jnp.where(row == 0, prev[hr - 2:hr - 1, :],
                            jnp.where(row == 1, prev[hr - 1:hr, :], pltpu.roll(u, 2, 0)))
            za_ref[r * sub_m // 2:(r + 1) * sub_m // 2, :] = _pack_rows(gate * _short_conv(u, um1, um2, w))
            halo_ref[...] = u[sub_m - hr:sub_m, :]
        nbp_ref[...] = u[sub_m - (CONV_WIDTH - 1):sub_m, :]

    @pl.when(i >= prompt_tiles)
    def _():
        nsq = sub_m // SAMPLE_LEN
        for r in range(tm // sub_m):
            gate, u = project(r)
            st = st_ref[r * nsq:(r + 1) * nsq]
            rows = lambda x: jnp.broadcast_to(x, (nsq, SAMPLE_LEN, tc)).reshape(sub_m, tc)
            s0, s1 = rows(st[:, 0:1, :]), rows(st[:, 1:2, :])
            pos = lax.broadcasted_iota(jnp.int32, u.shape, 0) & (SAMPLE_LEN - 1)
            um1 = jnp.where(pos == 0, s1, pltpu.roll(u, 1, 0))
            um2 = jnp.where(pos == 0, s0, jnp.where(pos == 1, s1, pltpu.roll(u, 2, 0)))
            za_ref[r * sub_m // 2:(r + 1) * sub_m // 2, :] = _pack_rows(gate * _short_conv(u, um1, um2, w))
            nbs_ref[r * nsq:(r + 1) * nsq] = u.reshape(nsq, SAMPLE_LEN, tc)[:, SAMPLE_LEN - (CONV_WIDTH - 1):, :]


def _inproj_conv(h, w_in_t, conv_w, state, layer, woffs, batch, seq, tm, tc):
    m, d = 2 * h.shape[0], h.shape[1]
    nsamp, cd = state.shape[1], state.shape[3]
    pt = batch * seq // tm
    tiles_per_seq = seq // tm
    sps = tm // SAMPLE_LEN
    sub_m = min(tm, 512)
    kern = functools.partial(_inproj_conv_kernel, prompt_tiles=pt, tiles_per_seq=tiles_per_seq, sub_m=sub_m)
    w_spec = lambda name: pl.BlockSpec((None, tc, d), lambda j, i: (layer, woffs[name] // tc + j, 0))
    samp = lambda i: jnp.maximum(i - pt, 0)
    return pl.pallas_call(
        kern,
        grid=(cd // tc, m // tm),
        in_specs=[pl.BlockSpec((tm // 2, d), lambda j, i: (i, 0)),
                  w_spec("cB"), w_spec("cC"), w_spec("cx"), w_spec("cg"),
                  pl.BlockSpec((None, CONV_WIDTH, tc), lambda j, i: (layer, 0, j)),
                  pl.BlockSpec((None, sps, CONV_WIDTH - 1, tc), lambda j, i: (layer, samp(i), 0, j))],
        out_specs=[pl.BlockSpec((tm // 2, tc), lambda j, i: (i, j)),
                   pl.BlockSpec((None, CONV_WIDTH - 1, tc),
                                lambda j, i: (jnp.minimum(i // tiles_per_seq, batch - 1), 0, j)),
                   pl.BlockSpec((sps, CONV_WIDTH - 1, tc), lambda j, i: (samp(i), 0, j))],
        out_shape=[jax.ShapeDtypeStruct((m // 2, cd), jnp.uint32),
                   jax.ShapeDtypeStruct((batch, CONV_WIDTH - 1, cd), F32),
                   jax.ShapeDtypeStruct((nsamp, CONV_WIDTH - 1, cd), F32)],
        scratch_shapes=[pltpu.VMEM((d, 4 * tc), BF16), pltpu.VMEM((8, tc), F32)],
        compiler_params=_cparams("arbitrary", "arbitrary"),
        name="inproj_conv",
    )(h, w_in_t, w_in_t, w_in_t, w_in_t, conv_w, state)


def _branch_kernel(za_ref, zbp_ref, zbs_ref, wa_ref, wb_ref, sga_ref, sgb_ref, o_ref, wa_bf, wb_bf, *, prompt_tiles):
    i = pl.program_id(1)

    @pl.when(i == 0)
    def _():
        wa_bf[...] = wa_ref[...].astype(BF16)
        wb_bf[...] = wb_ref[...].astype(BF16)

    def merge(zb_ref):
        tn = o_ref.shape[1]
        sub = min(tn, MXU_WIDTH)
        for c in range(tn // sub):
            cols = slice(c * sub, (c + 1) * sub)
            a = _dot(_unpack_rows(za_ref[...]), wa_bf[:, cols])
            b = _dot(_unpack_rows(zb_ref[...]), wb_bf[:, cols])
            o_ref[:, cols] = _pack_rows(sga_ref[:, cols].astype(F32) * a + sgb_ref[:, cols].astype(F32) * b)

    @pl.when(i < prompt_tiles)
    def _():
        merge(zbp_ref)

    @pl.when(i >= prompt_tiles)
    def _():
        merge(zbs_ref)


def _branch(za, zb_p, zb_s, w_a, w_b, p, layer, offs, tm, tn):
    m, cd = 2 * za.shape[0], za.shape[1]
    dvt = zb_p.shape[1]
    d = w_a.shape[2]
    pt = 2 * zb_p.shape[0] // tm
    gao, gbo = offs["ga"] // tn, offs["gb"] // tn
    return pl.pallas_call(
        functools.partial(_branch_kernel, prompt_tiles=pt),
        grid=(d // tn, m // tm),
        in_specs=[pl.BlockSpec((tm // 2, cd), lambda j, i: (i, 0)),
                  pl.BlockSpec((tm // 2, dvt), lambda j, i: (jnp.minimum(i, pt - 1), 0)),
                  pl.BlockSpec((tm // 2, dvt), lambda j, i: (jnp.maximum(i - pt, 0), 0)),
                  pl.BlockSpec((None, cd, tn), lambda j, i: (layer, 0, j)),
                  pl.BlockSpec((None, dvt, tn), lambda j, i: (layer, 0, j)),
                  pl.BlockSpec((tm, tn), lambda j, i: (i, gao + j)),
                  pl.BlockSpec((tm, tn), lambda j, i: (i, gbo + j))],
        out_specs=pl.BlockSpec((tm // 2, tn), lambda j, i: (i, j)),
        out_shape=jax.ShapeDtypeStruct((m // 2, d), jnp.uint32),
        scratch_shapes=[pltpu.VMEM((cd, tn), BF16), pltpu.VMEM((dvt, tn), BF16)],
        compiler_params=_cparams("arbitrary", "arbitrary"),
        name="branch",
    )(za, zb_p, zb_s, w_a, w_b, p, p)


def _out_kernel(m_ref, w_ref, x_ref, g_ref, xo_ref, ho_ref, w_bf):
    @pl.when(pl.program_id(0) == 0)
    def _():
        w_bf[...] = w_ref[...].astype(BF16)

    x = x_ref[...] + _dot(_unpack_rows(m_ref[...]), w_bf[...])
    xo_ref[...] = x
    ho_ref[...] = _pack_rows(_rms(x, g_ref[...]))


def _out_final_kernel(m_ref, w_ref, x_ref, g_ref, yp_ref, ys_ref, w_bf, *, prompt_tiles):
    i = pl.program_id(0)

    @pl.when(i == 0)
    def _():
        w_bf[...] = w_ref[...].astype(BF16)

    y = _rms(x_ref[...] + _dot(_unpack_rows(m_ref[...]), w_bf[...]), g_ref[...])

    @pl.when(i < prompt_tiles)
    def _():
        yp_ref[...] = y

    @pl.when(i >= prompt_tiles)
    def _():
        ys_ref[...] = y


def _out(merged, w_out, x, g, layer, tm, prompt_rows=None):
    m, d = x.shape
    in_specs = [pl.BlockSpec((tm // 2, d), lambda i: (i, 0)),
                pl.BlockSpec((None, d, d), lambda i: (layer, 0, 0), pipeline_mode=pl.Buffered(1)),
                pl.BlockSpec((tm, d), lambda i: (i, 0)),
                pl.BlockSpec((1, d), lambda i: (0, 0))]
    if prompt_rows is None:
        kern = _out_kernel
        out_specs = [pl.BlockSpec((tm, d), lambda i: (i, 0)), pl.BlockSpec((tm // 2, d), lambda i: (i, 0))]
        out_shape = [jax.ShapeDtypeStruct((m, d), F32), jax.ShapeDtypeStruct((m // 2, d), jnp.uint32)]
    else:
        pt = prompt_rows // tm
        kern = functools.partial(_out_final_kernel, prompt_tiles=pt)
        out_specs = [pl.BlockSpec((tm, d), lambda i: (jnp.minimum(i, pt - 1), 0)),
                     pl.BlockSpec((tm, d), lambda i: (jnp.maximum(i - pt, 0), 0))]
        out_shape = [jax.ShapeDtypeStruct((prompt_rows, d), F32), jax.ShapeDtypeStruct((m - prompt_rows, d), F32)]
    return pl.pallas_call(
        kern,
        grid=(m // tm,),
        in_specs=in_specs,
        out_specs=out_specs,
        out_shape=out_shape,
        scratch_shapes=[pltpu.VMEM((d, d), BF16)],
        compiler_params=_cparams("arbitrary"),
        name="outproj",
    )(merged, w_out, x, g.reshape(1, d))


def kernel(x_prompt, x_sample, state_conv, state_gla, norm_g, w_in, conv_w, w_alpha2, b_alpha,
           gla_norm_g, w_branch_a, w_branch_b, w_out, final_norm_g):
    batch, seq, d = x_prompt.shape
    nsamp, slen, _ = x_sample.shape
    depth, _, _, cd = state_conv.shape
    _, _, heads, dk, dv = state_gla.shape
    rank, dkt = w_alpha2.shape[1:]
    dvt = heads * dv
    assert slen == SAMPLE_LEN and dkt == heads * dk and state_conv.shape[2] == CONV_WIDTH - 1
    widths = (("ga", d), ("gb", d), ("cB", cd), ("cC", cd), ("cx", cd), ("cg", cd),
              ("q", dkt), ("k", dkt), ("v", dvt), ("gg", dvt), ("lr", rank))
    woffs, o = {}, 0
    for name, wd in widths:
        woffs[name] = o
        o += wd
    assert o == w_in.shape[2] and rank <= LANES and woffs["lr"] % LANES == 0 and cd == d
    poffs, o = {}, 0
    for name, wd in widths:
        if name in ("ga", "gb", "q", "k", "v", "gg"):
            poffs[name] = o
            o += wd
    poffs["end"] = o

    mp, ms = batch * seq, nsamp * slen
    m = mp + ms
    tm = _tile(ms, 1024)
    tn = _tile(dkt, 1024)
    tm_wide = next((t for t in (1536, 1024) if m % t == 0), tm)
    chunk = _tile(seq, 256)
    sb = _tile(nsamp, 16)
    tblk = seq * batch * sb // nsamp
    assert seq % tblk == 0 and tblk % chunk == 0
    w2 = jnp.pad(w_alpha2, ((0, 0), (0, LANES - rank), (0, 0))).astype(BF16)
    w_in_t = jnp.swapaxes(w_in, 1, 2)

    x, h = _embed(x_prompt.reshape(mp, d), x_sample.reshape(ms, d), norm_g[0])
    conv_p, gla_p, conv_s, gla_s = [], [], [], None
    for l in range(depth):
        p = _inproj(h, w_in_t, l, woffs, poffs, dk, tm_wide, tn)
        za, nb_p, nb_s = _inproj_conv(h, w_in_t, conv_w, state_conv, l, woffs, batch, seq, tm, _tile(cd, MXU_WIDTH))
        la = _gate(h, w_in_t, w2, b_alpha, l, woffs["lr"], rank)
        zb_p, s_p, zb_s, gla_s = _gla(p, la, gla_norm_g, state_gla, gla_s, l, poffs, batch, seq, chunk, tblk, sb)
        merged = _branch(za, zb_p, zb_s, w_branch_a, w_branch_b, p, l, poffs, tm, _tile(d, 512))
        conv_p.append(nb_p)
        gla_p.append(s_p)
        conv_s.append(nb_s)
        if l < depth - 1:
            x, h = _out(merged, w_out, x, norm_g[l + 1], l, _tile(ms, 512))
        else:
            y_p, y_s = _out(merged, w_out, x, final_norm_g, l, _tile(ms, 256), prompt_rows=mp)
    return (y_p.reshape(batch, seq, d), y_s.reshape(nsamp, slen, d),
            jnp.stack(conv_p), jnp.stack(gla_p), jnp.stack(conv_s), gla_s)
```

```python
import functools

import jax
import jax.numpy as jnp
from jax import lax
from jax.experimental import pallas as pl
from jax.experimental.pallas import tpu as pltpu

F32 = jnp.float32
BF16 = jnp.bfloat16
NORM_EPS = 1e-6
GLA_TAU = 16.0
LOG2E = 1.4426950408889634
CONV_WIDTH = 3
SAMPLE_LEN = 8
LANES = 128
MXU_WIDTH = 256
VMEM_LIMIT = 58 << 20


def _cparams(*sem):
    return pltpu.CompilerParams(dimension_semantics=sem, vmem_limit_bytes=VMEM_LIMIT)


def _dot(a, b):
    return jnp.dot(a, b, preferred_element_type=F32)


def _dot_nt(a, b):
    return lax.dot_general(a, b, (((1,), (1,)), ((), ())), preferred_element_type=F32)


def _split3(x):
    hi = x.astype(BF16)
    r1 = x - hi.astype(F32)
    mid = r1.astype(BF16)
    lo = (r1 - mid.astype(F32)).astype(BF16)
    return hi, mid, lo


def _tile(n, pref):
    t = min(n, pref)
    while n % t:
        t //= 2
    return t


def _pack_rows(x):
    return pltpu.bitcast(x.astype(BF16), jnp.uint32)


def _unpack_rows(x):
    return pltpu.bitcast(x, BF16)


def _sigmoid(x):
    return 0.5 * jnp.tanh(0.5 * x) + 0.5


def _rms(x, g):
    ms = jnp.mean(x * x, axis=-1, keepdims=True)
    return x * lax.rsqrt(ms + NORM_EPS) * g


def _embed_kernel(xp_ref, xs_ref, g_ref, x_ref, h_ref, *, prompt_tiles):
    i = pl.program_id(0)

    def emit(x):
        x_ref[...] = x
        h_ref[...] = _pack_rows(_rms(x, g_ref[...]))

    @pl.when(i < prompt_tiles)
    def _():
        emit(xp_ref[...])

    @pl.when(i >= prompt_tiles)
    def _():
        emit(xs_ref[...])


def _embed(xp, xs, g):
    (mp, d), ms = xp.shape, xs.shape[0]
    tm = _tile(ms, 512)
    pt = mp // tm
    kern = functools.partial(_embed_kernel, prompt_tiles=pt)
    return pl.pallas_call(
        kern,
        grid=((mp + ms) // tm,),
        in_specs=[pl.BlockSpec((tm, d), lambda i: (jnp.minimum(i, pt - 1), 0)),
                  pl.BlockSpec((tm, d), lambda i: (jnp.maximum(i - pt, 0), 0)),
                  pl.BlockSpec((1, d), lambda i: (0, 0))],
        out_specs=[pl.BlockSpec((tm, d), lambda i: (i, 0)),
                   pl.BlockSpec((tm // 2, d), lambda i: (i, 0))],
        out_shape=[jax.ShapeDtypeStruct((mp + ms, d), F32),
                   jax.ShapeDtypeStruct(((mp + ms) // 2, d), jnp.uint32)],
        compiler_params=_cparams("arbitrary"),
        name="embed",
    )(xp, xs, g.reshape(1, d))


def _inproj_kernel(h_ref, w_ref, o_ref, wb_ref, *, sig_end, silu_lo, q_lo, q_hi, q_scale):
    j = pl.program_id(0)

    @pl.when(pl.program_id(1) == 0)
    def _():
        wb_ref[...] = w_ref[...].T.astype(BF16)

    is_sig = j < sig_end
    is_silu = j >= silu_lo
    is_q = (j >= q_lo) & (j < q_hi)
    gated = is_sig | is_silu
    alpha = jnp.where(gated, 0.5, jnp.where(is_q, q_scale, 1.0)).astype(F32)
    beta = jnp.where(gated, 0.5, 0.0).astype(F32)
    tm, tn = o_ref.shape
    sub_n = min(tn, 2 * MXU_WIDTH)
    sub_m = next((s for s in (768, 1024, 512) if tm % s == 0), tm)
    for c in range(tn // sub_n):
        cols = slice(c * sub_n, (c + 1) * sub_n)
        for r in range(tm // sub_m):
            acc = _dot(_unpack_rows(h_ref[r * sub_m // 2:(r + 1) * sub_m // 2, :]), wb_ref[:, cols])
            gate = alpha + beta * jnp.tanh(0.5 * acc)
            o_ref[r * sub_m:(r + 1) * sub_m, cols] = (jnp.where(is_sig, 1.0, acc) * gate).astype(o_ref.dtype)


def _inproj(h, w_in_t, layer, woffs, poffs, dk, tm, tn):
    m, d = 2 * h.shape[0], h.shape[1]
    n = poffs["end"]
    gate_tiles = poffs["q"] // tn
    skip_tiles = (woffs["q"] - woffs["cB"]) // tn
    kern = functools.partial(
        _inproj_kernel,
        sig_end=gate_tiles, silu_lo=poffs["gg"] // tn,
        q_lo=poffs["q"] // tn, q_hi=poffs["k"] // tn, q_scale=float(dk) ** -0.5)
    w_tile = lambda j: jnp.where(j < gate_tiles, j, j + skip_tiles)
    return pl.pallas_call(
        kern,
        grid=(n // tn, m // tm),
        in_specs=[pl.BlockSpec((tm // 2, d), lambda j, i: (i, 0)),
                  pl.BlockSpec((None, tn, d), lambda j, i: (layer, w_tile(j), 0))],
        out_specs=pl.BlockSpec((tm, tn), lambda j, i: (i, j)),
        out_shape=jax.ShapeDtypeStruct((m, n), BF16),
        scratch_shapes=[pltpu.VMEM((d, tn), BF16)],
        compiler_params=_cparams("arbitrary", "arbitrary"),
        name="inproj",
    )(h, w_in_t)


def _log_decay(h, wlr_ref, w2_ref, ba_ref, rank):
    lr = _dot_nt(h, wlr_ref[...].astype(BF16))
    lr = jnp.where(lax.broadcasted_iota(jnp.int32, lr.shape, 1) < rank, lr, 0.0)
    logit = _dot(lr.astype(BF16), w2_ref[...]) + ba_ref[...]
    log_sig = jnp.minimum(logit, 0.0) - jnp.log(1.0 + jnp.exp(-jnp.abs(logit)))
    return log_sig / GLA_TAU


def _block_row(x, period, idx):
    c, w = x.shape
    x3 = x.reshape(c // period, period, w)
    return jnp.broadcast_to(x3[:, idx:idx + 1, :], x3.shape).reshape(c, w)


def _level_factor(q, k, b2, la2, m):
    c, w = b2.shape
    pos = lax.broadcasted_iota(jnp.int32, (c, w), 0) & (2 * m - 1)
    upper = pos >= m
    if 2 * m >= 8:
        e = -jnp.abs(b2 - _block_row(b2, 2 * m, m - 1))
    elif m == 1:
        e = jnp.where(upper, la2, 0.0)
    else:
        assert m == 2
        sub = lax.broadcasted_iota(jnp.int32, (c, w), 0) & 7
        e = -jnp.abs(b2 - jnp.where(sub < 4, _block_row(b2, 8, 1), _block_row(b2, 8, 5)))
    return (jnp.where(upper, q, k) * jnp.exp2(e)).astype(BF16)


def _pair_code(n):
    row = lax.broadcasted_iota(jnp.int32, (n, n), 0)
    col = lax.broadcasted_iota(jnp.int32, (n, n), 1)
    xf = (row ^ col).astype(F32)
    high_bit = (lax.bitcast_convert_type(xf, jnp.int32) >> 23) - 127
    return jnp.where(col < row, high_bit, jnp.where(col == row, -1, -2))


def _diag_scores(q, k, b2, la2, levels, code, blocks):
    qb, kb = q.astype(BF16), k.astype(BF16)
    out = [jnp.where(code == -1, _dot_nt(qb[lo:hi], kb[lo:hi]), 0.0) for lo, hi in blocks]
    for m in levels:
        g = _level_factor(q, k, b2, la2, m)
        bit = m.bit_length() - 1
        out = [jnp.where(code == bit, _dot_nt(g[lo:hi], g[lo:hi]), a) for a, (lo, hi) in zip(out, blocks)]
    return out


def _head_norm_gate(o, gn, gg):
    return _rms(o, gn) * gg


def _levels(c):
    out, m = [], c // 2
    while m >= 1:
        out.append(m)
        m //= 2
    return tuple(out)


def _gla_prompt_body(q_ref, k_ref, v_ref, gg_ref, la_ref, gn_ref, zb_ref, s_ref, tril_ref, code_ref, *, chunk,
                     first_block):
    dk = q_ref.shape[1]
    c, hc = chunk, chunk // 2

    @pl.when(first_block)
    def _():
        s_ref[...] = jnp.zeros_like(s_ref)

    row = lax.broadcasted_iota(jnp.int32, (c, c), 0)
    col = lax.broadcasted_iota(jnp.int32, (c, c), 1)
    tril_ref[...] = (col <= row).astype(BF16)
    code_ref[...] = _pair_code(hc)

    def step(ci, carry):
        rows = pl.ds(pl.multiple_of(ci * c, c), c)
        la = la_ref[rows, :]
        tril = tril_ref[...]
        hi, mid, lo = _split3(la)
        b2 = (_dot(tril, hi) + _dot(tril, mid) + _dot(tril, lo)) * LOG2E
        la2 = la * LOG2E
        b2_last = b2[c - 1:c, :]

        q = q_ref[rows, :].astype(F32)
        k = k_ref[rows, :].astype(F32)
        v = v_ref[rows, :]
        levels = _levels(c)
        a00, a11 = _diag_scores(q, k, b2, la2, levels[1:], code_ref[...], ((0, hc), (hc, c)))
        g_top = _level_factor(q, k, b2, la2, hc)
        a10 = _dot_nt(g_top[hc:], g_top[:hc])
        s_old = s_ref[...]
        o_intra = jnp.concatenate(
            [_dot(a00.astype(BF16), v[:hc]),
             _dot(jnp.concatenate([a10, a11], axis=1).astype(BF16), v)], axis=0)
        o = o_intra + _dot((q * jnp.exp2(b2)).astype(BF16), s_old.astype(BF16))
        packed_rows = pl.ds(pl.multiple_of(ci * hc, hc), hc)
        zb_ref[packed_rows, :] = _pack_rows(_head_norm_gate(o, gn_ref[...], gg_ref[rows, :].astype(F32)))

        kd_t = (k * jnp.exp2(b2_last - b2)).T.astype(BF16)
        eye = lax.broadcasted_iota(jnp.int32, (dk, dk), 0) == lax.broadcasted_iota(jnp.int32, (dk, dk), 1)
        d_col = jnp.sum(jnp.where(eye, jnp.exp2(b2_last), 0.0), axis=1, keepdims=True)
        s_ref[...] = d_col * s_old + _dot(kd_t, v)
        return carry

    lax.fori_loop(0, q_ref.shape[0] // c, step, 0)


def _gla_sample_body(q_ref, k_ref, v_ref, gg_ref, la_ref, gn_ref, sin_ref, zb_ref, sout_ref, o_scr):
    r, dk = q_ref.shape
    nseq = r // SAMPLE_LEN
    la = la_ref[...]
    row = lax.broadcasted_iota(jnp.int32, (r, r), 0)
    col = lax.broadcasted_iota(jnp.int32, (r, r), 1)
    tril = ((col <= row) & ((row ^ col) < SAMPLE_LEN)).astype(BF16)
    hi, mid, lo = _split3(la)
    b2 = (_dot(tril, hi) + _dot(tril, mid) + _dot(tril, lo)) * LOG2E
    b2_last = _block_row(b2, SAMPLE_LEN, SAMPLE_LEN - 1)

    q = q_ref[...].astype(F32)
    k = k_ref[...].astype(F32)
    v = v_ref[...]
    (a,) = _diag_scores(q, k, b2, la * LOG2E, _levels(SAMPLE_LEN), _pair_code(r), ((0, r),))
    o_scr[...] = _dot(a.astype(BF16), v)
    decay = jnp.exp2(b2)
    qe = (q * decay).astype(BF16)
    kd_t = (k * jnp.exp2(b2_last - b2)).T
    d_t = decay.T
    lane_seq = lax.broadcasted_iota(jnp.int32, (dk, r), 1) // SAMPLE_LEN
    for s in range(nseq):
        lo_r, hi_r = s * SAMPLE_LEN, (s + 1) * SAMPLE_LEN
        s_old = sin_ref[s]
        o_scr[lo_r:hi_r, :] += _dot(qe[lo_r:hi_r, :], s_old.astype(BF16))
        kd_s = jnp.where(lane_seq == s, kd_t, 0.0).astype(BF16)
        sout_ref[s] = d_t[:, hi_r - 1:hi_r] * s_old + _dot(kd_s, v)
    zb_ref[...] = _pack_rows(_head_norm_gate(o_scr[...], gn_ref[...], gg_ref[...].astype(F32)))


def _gla_kernel(*refs, chunk, blocks_per_seq, n_alias):
    (qp, kp, vp, ggp, lap, qs, ks, vs, ggs, las, gn, sin), rest = refs[:12], refs[12 + n_alias:]
    zbp, sp, zbs, sout, tril_ref, code_ref, o_scr = rest
    first_block = pl.program_id(0) % blocks_per_seq == 0
    _gla_prompt_body(qp, kp, vp, ggp, lap, gn, zbp, sp, tril_ref, code_ref, chunk=chunk, first_block=first_block)
    _gla_sample_body(qs, ks, vs, ggs, las, gn, sin, zbs, sout, o_scr)


def _gla(p, la, gn, state, new_state, layer, offs, batch, seq, chunk, tblk, sb):
    _, nseq, heads, dk, dv = state.shape
    nb = seq // tblk
    steps = batch * heads * nb
    assert steps == (nseq // sb) * heads
    r = sb * SAMPLE_LEN
    rb0 = batch * seq // r
    qb, kb, vb, gb = offs["q"] // dk, offs["k"] // dk, offs["v"] // dv, offs["gg"] // dv
    prow = lambda s: (s // (heads * nb)) * nb + s % nb
    phead = lambda s: (s // nb) % heads
    srow = lambda s: rb0 + s // heads
    shead = lambda s: s % heads
    st_spec = pl.BlockSpec((None, sb, None, dk, dv), lambda s: (layer, s // heads, shead(s), 0, 0))
    in_specs = [pl.BlockSpec((tblk, dk), lambda s: (prow(s), qb + phead(s))),
                pl.BlockSpec((tblk, dk), lambda s: (prow(s), kb + phead(s))),
                pl.BlockSpec((tblk, dv), lambda s: (prow(s), vb + phead(s))),
                pl.BlockSpec((tblk, dv), lambda s: (prow(s), gb + phead(s))),
                pl.BlockSpec((tblk, dk), lambda s: (prow(s), phead(s))),
                pl.BlockSpec((r, dk), lambda s: (srow(s), qb + shead(s))),
                pl.BlockSpec((r, dk), lambda s: (srow(s), kb + shead(s))),
                pl.BlockSpec((r, dv), lambda s: (srow(s), vb + shead(s))),
                pl.BlockSpec((r, dv), lambda s: (srow(s), gb + shead(s))),
                pl.BlockSpec((r, dk), lambda s: (srow(s), shead(s))),
                pl.BlockSpec((None, 1, dv), lambda s: (layer, 0, 0)),
                st_spec]
    args = [p] * 4 + [la] + [p] * 4 + [la, gn.reshape(gn.shape[0], 1, dv), state]
    aliases = {}
    if new_state is not None:
        in_specs.append(pl.BlockSpec(memory_space=pl.ANY))
        args.append(new_state)
        aliases[len(args) - 1] = 3
    return pl.pallas_call(
        functools.partial(_gla_kernel, chunk=chunk, blocks_per_seq=nb, n_alias=len(aliases)),
        grid=(steps,),
        in_specs=in_specs,
        out_specs=[pl.BlockSpec((tblk // 2, dv), lambda s: (prow(s), phead(s))),
                   pl.BlockSpec((None, None, dk, dv), lambda s: (s // (heads * nb), phead(s), 0, 0)),
                   pl.BlockSpec((r // 2, dv), lambda s: (s // heads, shead(s))),
                   st_spec],
        out_shape=[jax.ShapeDtypeStruct((batch * seq // 2, heads * dv), jnp.uint32),
                   jax.ShapeDtypeStruct((batch, heads, dk, dv), F32),
                   jax.ShapeDtypeStruct((nseq * SAMPLE_LEN // 2, heads * dv), jnp.uint32),
                   jax.ShapeDtypeStruct(state.shape, F32)],
        scratch_shapes=[pltpu.VMEM((chunk, chunk), BF16), pltpu.VMEM((chunk // 2, chunk // 2), jnp.int32),
                        pltpu.VMEM((r, dv), F32)],
        input_output_aliases=aliases,
        compiler_params=_cparams("arbitrary"),
        name="gla",
    )(*args)


def _short_conv(u, um1, um2, w):
    return um2 * w[0:1, :] + um1 * w[1:2, :] + u * w[2:3, :]


def _inproj_conv_kernel(h_ref, wcb_ref, wcc_ref, wcx_ref, wcg_ref, cw_ref, st_ref, wlr_ref, w2_ref, ba_ref,
                        za_ref, nbp_ref, nbs_ref, la_ref, wb_ref, halo_ref, *, prompt_tiles, tiles_per_seq, sub_m,
                        rank):
    i = pl.program_id(1)
    tc = wcb_ref.shape[0]
    tm = 2 * h_ref.shape[0]

    @pl.when(pl.program_id(0) == 0)
    def _():
        for r in range(tm // sub_m):
            hh = _unpack_rows(h_ref[r * sub_m // 2:(r + 1) * sub_m // 2, :])
            la_ref[r * sub_m:(r + 1) * sub_m, :] = _log_decay(hh, wlr_ref, w2_ref, ba_ref, rank)

    @pl.when(i == 0)
    def _():
        for g, ref in enumerate((wcb_ref, wcc_ref, wcx_ref, wcg_ref)):
            wb_ref[:, g * tc:(g + 1) * tc] = ref[...].T.astype(BF16)

    w = cw_ref[...]

    def project(r):
        hh = _unpack_rows(h_ref[r * sub_m // 2:(r + 1) * sub_m // 2, :])
        bc = _dot(hh, wb_ref[:, 0:2 * tc])
        xg = _dot(hh, wb_ref[:, 2 * tc:4 * tc])
        cg = xg[:, tc:]
        return bc[:, :tc] * (cg * _sigmoid(cg)), bc[:, tc:] * xg[:, :tc]

    @pl.when(i < prompt_tiles)
    def _():
        @pl.when(i % tiles_per_seq == 0)
        def _():
            halo_ref[...] = jnp.zeros_like(halo_ref)

        for r in range(tm // sub_m):
            gate, u = project(r)
            hr = halo_ref.shape[0]
            prev = halo_ref[...]
            row = lax.broadcasted_iota(jnp.int32, u.shape, 0)
            um1 = jnp.where(row == 0, prev[hr - 1:hr, :], pltpu.roll(u, 1, 0))
            um2 = jnp.where(row == 0, prev[hr - 2:hr - 1, :],
                            jnp.where(row == 1, prev[hr - 1:hr, :], pltpu.roll(u, 2, 0)))
            za_ref[r * sub_m // 2:(r + 1) * sub_m // 2, :] = _pack_rows(gate * _short_conv(u, um1, um2, w))
            halo_ref[...] = u[sub_m - hr:sub_m, :]
        nbp_ref[...] = u[sub_m - (CONV_WIDTH - 1):sub_m, :]

    @pl.when(i >= prompt_tiles)
    def _():
        nsq = sub_m // SAMPLE_LEN
        for r in range(tm // sub_m):
            gate, u = project(r)
            st = st_ref[r * nsq:(r + 1) * nsq]
            rows = lambda x: jnp.broadcast_to(x, (nsq, SAMPLE_LEN, tc)).reshape(sub_m, tc)
            s0, s1 = rows(st[:, 0:1, :]), rows(st[:, 1:2, :])
            pos = lax.broadcasted_iota(jnp.int32, u.shape, 0) & (SAMPLE_LEN - 1)
            um1 = jnp.where(pos == 0, s1, pltpu.roll(u, 1, 0))
            um2 = jnp.where(pos == 0, s0, jnp.where(pos == 1, s1, pltpu.roll(u, 2, 0)))
            za_ref[r * sub_m // 2:(r + 1) * sub_m // 2, :] = _pack_rows(gate * _short_conv(u, um1, um2, w))
            nbs_ref[r * nsq:(r + 1) * nsq] = u.reshape(nsq, SAMPLE_LEN, tc)[:, SAMPLE_LEN - (CONV_WIDTH - 1):, :]


def _inproj_conv(h, w_in_t, conv_w, state, w2, ba, layer, woffs, rank, batch, seq, tm, tc):
    m, d = 2 * h.shape[0], h.shape[1]
    nsamp, cd = state.shape[1], state.shape[3]
    rp, dkt = w2.shape[1:]
    last = m // tm - 1
    pt = batch * seq // tm
    tiles_per_seq = seq // tm
    sps = tm // SAMPLE_LEN
    sub_m = min(tm, 512)
    kern = functools.partial(_inproj_conv_kernel, prompt_tiles=pt, tiles_per_seq=tiles_per_seq, sub_m=sub_m,
                             rank=rank)
    w_spec = lambda name: pl.BlockSpec((None, tc, d), lambda j, i: (layer, woffs[name] // tc + j, 0))
    samp = lambda i: jnp.maximum(i - pt, 0)
    return pl.pallas_call(
        kern,
        grid=(cd // tc, m // tm),
        in_specs=[pl.BlockSpec((tm // 2, d), lambda j, i: (i, 0)),
                  w_spec("cB"), w_spec("cC"), w_spec("cx"), w_spec("cg"),
                  pl.BlockSpec((None, CONV_WIDTH, tc), lambda j, i: (layer, 0, j)),
                  pl.BlockSpec((None, sps, CONV_WIDTH - 1, tc), lambda j, i: (layer, samp(i), 0, j)),
                  pl.BlockSpec((None, rp, d), lambda j, i: (layer, woffs["lr"] // rp, 0)),
                  pl.BlockSpec((None, rp, dkt), lambda j, i: (layer, 0, 0)),
                  pl.BlockSpec((None, 1, dkt), lambda j, i: (layer, 0, 0))],
        out_specs=[pl.BlockSpec((tm // 2, tc), lambda j, i: (i, j)),
                   pl.BlockSpec((None, CONV_WIDTH - 1, tc),
                                lambda j, i: (jnp.minimum(i // tiles_per_seq, batch - 1), 0, j)),
                   pl.BlockSpec((sps, CONV_WIDTH - 1, tc), lambda j, i: (samp(i), 0, j)),
                   pl.BlockSpec((tm, dkt), lambda j, i: (jnp.where(j == 0, i, last), 0))],
        out_shape=[jax.ShapeDtypeStruct((m // 2, cd), jnp.uint32),
                   jax.ShapeDtypeStruct((batch, CONV_WIDTH - 1, cd), F32),
                   jax.ShapeDtypeStruct((nsamp, CONV_WIDTH - 1, cd), F32),
                   jax.ShapeDtypeStruct((m, dkt), F32)],
        scratch_shapes=[pltpu.VMEM((d, 4 * tc), BF16), pltpu.VMEM((8, tc), F32)],
        compiler_params=_cparams("arbitrary", "arbitrary"),
        name="inproj_conv",
    )(h, w_in_t, w_in_t, w_in_t, w_in_t, conv_w, state, w_in_t, w2, ba.reshape(ba.shape[0], 1, dkt))


def _branch_kernel(za_ref, zbp_ref, zbs_ref, wa_ref, wb_ref, sga_ref, sgb_ref, o_ref, wa_bf, wb_bf, *, prompt_tiles):
    i = pl.program_id(1)

    @pl.when(i == 0)
    def _():
        wa_bf[...] = wa_ref[...].astype(BF16)
        wb_bf[...] = wb_ref[...].astype(BF16)

    def merge(zb_ref):
        tn = o_ref.shape[1]
        sub = min(tn, MXU_WIDTH)
        for c in range(tn // sub):
            cols = slice(c * sub, (c + 1) * sub)
            a = _dot(_unpack_rows(za_ref[...]), wa_bf[:, cols])
            b = _dot(_unpack_rows(zb_ref[...]), wb_bf[:, cols])
            o_ref[:, cols] = _pack_rows(sga_ref[:, cols].astype(F32) * a + sgb_ref[:, cols].astype(F32) * b)

    @pl.when(i < prompt_tiles)
    def _():
        merge(zbp_ref)

    @pl.when(i >= prompt_tiles)
    def _():
        merge(zbs_ref)


def _branch(za, zb_p, zb_s, w_a, w_b, p, layer, offs, tm, tn):
    m, cd = 2 * za.shape[0], za.shape[1]
    dvt = zb_p.shape[1]
    d = w_a.shape[2]
    pt = 2 * zb_p.shape[0] // tm
    gao, gbo = offs["ga"] // tn, offs["gb"] // tn
    return pl.pallas_call(
        functools.partial(_branch_kernel, prompt_tiles=pt),
        grid=(d // tn, m // tm),
        in_specs=[pl.BlockSpec((tm // 2, cd), lambda j, i: (i, 0)),
                  pl.BlockSpec((tm // 2, dvt), lambda j, i: (jnp.minimum(i, pt - 1), 0)),
                  pl.BlockSpec((tm // 2, dvt), lambda j, i: (jnp.maximum(i - pt, 0), 0)),
                  pl.BlockSpec((None, cd, tn), lambda j, i: (layer, 0, j)),
                  pl.BlockSpec((None, dvt, tn), lambda j, i: (layer, 0, j)),
                  pl.BlockSpec((tm, tn), lambda j, i: (i, gao + j)),
                  pl.BlockSpec((tm, tn), lambda j, i: (i, gbo + j))],
        out_specs=pl.BlockSpec((tm // 2, tn), lambda j, i: (i, j)),
        out_shape=jax.ShapeDtypeStruct((m // 2, d), jnp.uint32),
        scratch_shapes=[pltpu.VMEM((cd, tn), BF16), pltpu.VMEM((dvt, tn), BF16)],
        compiler_params=_cparams("arbitrary", "arbitrary"),
        name="branch",
    )(za, zb_p, zb_s, w_a, w_b, p, p)


def _out_kernel(m_ref, w_ref, x_ref, g_ref, xo_ref, ho_ref, w_bf):
    @pl.when(pl.program_id(0) == 0)
    def _():
        w_bf[...] = w_ref[...].astype(BF16)

    x = x_ref[...] + _dot(_unpack_rows(m_ref[...]), w_bf[...])
    xo_ref[...] = x
    ho_ref[...] = _pack_rows(_rms(x, g_ref[...]))


def _out_final_kernel(m_ref, w_ref, x_ref, g_ref, yp_ref, ys_ref, w_bf, *, prompt_tiles):
    i = pl.program_id(0)

    @pl.when(i == 0)
    def _():
        w_bf[...] = w_ref[...].astype(BF16)

    y = _rms(x_ref[...] + _dot(_unpack_rows(m_ref[...]), w_bf[...]), g_ref[...])

    @pl.when(i < prompt_tiles)
    def _():
        yp_ref[...] = y

    @pl.when(i >= prompt_tiles)
    def _():
        ys_ref[...] = y


def _out(merged, w_out, x, g, layer, tm, prompt_rows=None):
    m, d = x.shape
    in_specs = [pl.BlockSpec((tm // 2, d), lambda i: (i, 0)),
                pl.BlockSpec((None, d, d), lambda i: (layer, 0, 0), pipeline_mode=pl.Buffered(1)),
                pl.BlockSpec((tm, d), lambda i: (i, 0)),
                pl.BlockSpec((1, d), lambda i: (0, 0))]
    if prompt_rows is None:
        kern = _out_kernel
        out_specs = [pl.BlockSpec((tm, d), lambda i: (i, 0)), pl.BlockSpec((tm // 2, d), lambda i: (i, 0))]
        out_shape = [jax.ShapeDtypeStruct((m, d), F32), jax.ShapeDtypeStruct((m // 2, d), jnp.uint32)]
    else:
        pt = prompt_rows // tm
        kern = functools.partial(_out_final_kernel, prompt_tiles=pt)
        out_specs = [pl.BlockSpec((tm, d), lambda i: (jnp.minimum(i, pt - 1), 0)),
                     pl.BlockSpec((tm, d), lambda i: (jnp.maximum(i - pt, 0), 0))]
        out_shape = [jax.ShapeDtypeStruct((prompt_rows, d), F32), jax.ShapeDtypeStruct((m - prompt_rows, d), F32)]
    return pl.pallas_call(
        kern,
        grid=(m // tm,),
        in_specs=in_specs,
        out_specs=out_specs,
        out_shape=out_shape,
        scratch_shapes=[pltpu.VMEM((d, d), BF16)],
        compiler_params=_cparams("arbitrary"),
        name="outproj",
    )(merged, w_out, x, g.reshape(1, d))


def kernel(x_prompt, x_sample, state_conv, state_gla, norm_g, w_in, conv_w, w_alpha2, b_alpha,
           gla_norm_g, w_branch_a, w_branch_b, w_out, final_norm_g):
    batch, seq, d = x_prompt.shape
    nsamp, slen, _ = x_sample.shape
    depth, _, _, cd = state_conv.shape
    _, _, heads, dk, dv = state_gla.shape
    rank, dkt = w_alpha2.shape[1:]
    dvt = heads * dv
    assert slen == SAMPLE_LEN and dkt == heads * dk and state_conv.shape[2] == CONV_WIDTH - 1
    widths = (("ga", d), ("gb", d), ("cB", cd), ("cC", cd), ("cx", cd), ("cg", cd),
              ("q", dkt), ("k", dkt), ("v", dvt), ("gg", dvt), ("lr", rank))
    woffs, o = {}, 0
    for name, wd in widths:
        woffs[name] = o
        o += wd
    assert o == w_in.shape[2] and rank <= LANES and woffs["lr"] % LANES == 0 and cd == d
    poffs, o = {}, 0
    for name, wd in widths:
        if name in ("ga", "gb", "q", "k", "v", "gg"):
            poffs[name] = o
            o += wd
    poffs["end"] = o

    mp, ms = batch * seq, nsamp * slen
    m = mp + ms
    tm = _tile(ms, 1024)
    tn = _tile(dkt, 1024)
    tm_wide = next((t for t in (1536, 1024) if m % t == 0), tm)
    chunk = _tile(seq, 256)
    sb = _tile(nsamp, 16)
    tblk = seq * batch * sb // nsamp
    assert seq % tblk == 0 and tblk % chunk == 0
    w2 = jnp.pad(w_alpha2, ((0, 0), (0, LANES - rank), (0, 0))).astype(BF16)
    w_in_t = jnp.swapaxes(w_in, 1, 2)

    x, h = _embed(x_prompt.reshape(mp, d), x_sample.reshape(ms, d), norm_g[0])
    conv_p, gla_p, conv_s, gla_s = [], [], [], None
    for l in range(depth):
        p = _inproj(h, w_in_t, l, woffs, poffs, dk, tm_wide, tn)
        za, nb_p, nb_s, la = _inproj_conv(h, w_in_t, conv_w, state_conv, w2, b_alpha, l, woffs, rank, batch, seq,
                                          tm, _tile(cd, MXU_WIDTH))
        zb_p, s_p, zb_s, gla_s = _gla(p, la, gla_norm_g, state_gla, gla_s, l, poffs, batch, seq, chunk, tblk, sb)
        merged = _branch(za, zb_p, zb_s, w_branch_a, w_branch_b, p, l, poffs, tm, _tile(d, 512))
        conv_p.append(nb_p)
        gla_p.append(s_p)
        conv_s.append(nb_s)
        if l < depth - 1:
            x, h = _out(merged, w_out, x, norm_g[l + 1], l, _tile(ms, 512))
        else:
            y_p, y_s = _out(merged, w_out, x, final_norm_g, l, _tile(ms, 512), prompt_rows=mp)
    return (y_p.reshape(batch, seq, d), y_s.reshape(nsamp, slen, d),
            jnp.stack(conv_p), jnp.stack(gla_p), jnp.stack(conv_s), gla_s)
```

```python
import functools

import jax
import jax.numpy as jnp
from jax import lax
from jax.experimental import pallas as pl
from jax.experimental.pallas import tpu as pltpu

F32 = jnp.float32
BF16 = jnp.bfloat16
NORM_EPS = 1e-6
GLA_TAU = 16.0
LOG2E = 1.4426950408889634
CONV_WIDTH = 3
SAMPLE_LEN = 8
LANES = 128
MXU_WIDTH = 256
VMEM_LIMIT = 58 << 20


def _cparams(*sem):
    return pltpu.CompilerParams(dimension_semantics=sem, vmem_limit_bytes=VMEM_LIMIT)


def _dot(a, b):
    return jnp.dot(a, b, preferred_element_type=F32)


def _dot_nt(a, b):
    return lax.dot_general(a, b, (((1,), (1,)), ((), ())), preferred_element_type=F32)


def _split3(x):
    hi = x.astype(BF16)
    r1 = x - hi.astype(F32)
    mid = r1.astype(BF16)
    lo = (r1 - mid.astype(F32)).astype(BF16)
    return hi, mid, lo


def _tile(n, pref):
    t = min(n, pref)
    while n % t:
        t //= 2
    return t


def _pack_rows(x):
    return pltpu.bitcast(x.astype(BF16), jnp.uint32)


def _unpack_rows(x):
    return pltpu.bitcast(x, BF16)


def _sigmoid(x):
    return 0.5 * jnp.tanh(0.5 * x) + 0.5


def _rms(x, g):
    ms = jnp.mean(x * x, axis=-1, keepdims=True)
    return x * lax.rsqrt(ms + NORM_EPS) * g


def _embed_kernel(xp_ref, xs_ref, g_ref, x_ref, h_ref, *, prompt_tiles):
    i = pl.program_id(0)

    def emit(x):
        x_ref[...] = x
        h_ref[...] = _pack_rows(_rms(x, g_ref[...]))

    @pl.when(i < prompt_tiles)
    def _():
        emit(xp_ref[...])

    @pl.when(i >= prompt_tiles)
    def _():
        emit(xs_ref[...])


def _embed(xp, xs, g):
    (mp, d), ms = xp.shape, xs.shape[0]
    tm = _tile(ms, 512)
    pt = mp // tm
    kern = functools.partial(_embed_kernel, prompt_tiles=pt)
    return pl.pallas_call(
        kern,
        grid=((mp + ms) // tm,),
        in_specs=[pl.BlockSpec((tm, d), lambda i: (jnp.minimum(i, pt - 1), 0)),
                  pl.BlockSpec((tm, d), lambda i: (jnp.maximum(i - pt, 0), 0)),
                  pl.BlockSpec((1, d), lambda i: (0, 0))],
        out_specs=[pl.BlockSpec((tm, d), lambda i: (i, 0)),
                   pl.BlockSpec((tm // 2, d), lambda i: (i, 0))],
        out_shape=[jax.ShapeDtypeStruct((mp + ms, d), F32),
                   jax.ShapeDtypeStruct(((mp + ms) // 2, d), jnp.uint32)],
        compiler_params=_cparams("arbitrary"),
        name="embed",
    )(xp, xs, g.reshape(1, d))


def _inproj_kernel(h_ref, w_ref, o_ref, wb_ref, *, sig_end, silu_lo, q_lo, q_hi, q_scale):
    j = pl.program_id(0)

    @pl.when(pl.program_id(1) == 0)
    def _():
        wb_ref[...] = w_ref[...].T.astype(BF16)

    is_sig = j < sig_end
    is_silu = j >= silu_lo
    is_q = (j >= q_lo) & (j < q_hi)
    gated = is_sig | is_silu
    alpha = jnp.where(gated, 0.5, jnp.where(is_q, q_scale, 1.0)).astype(F32)
    beta = jnp.where(gated, 0.5, 0.0).astype(F32)
    tm, tn = o_ref.shape
    sub_n = min(tn, 2 * MXU_WIDTH)
    sub_m = next((s for s in (768, 1024, 512) if tm % s == 0), tm)
    for c in range(tn // sub_n):
        cols = slice(c * sub_n, (c + 1) * sub_n)
        for r in range(tm // sub_m):
            acc = _dot(_unpack_rows(h_ref[r * sub_m // 2:(r + 1) * sub_m // 2, :]), wb_ref[:, cols])
            gate = alpha + beta * jnp.tanh(0.5 * acc)
            o_ref[r * sub_m:(r + 1) * sub_m, cols] = (jnp.where(is_sig, 1.0, acc) * gate).astype(o_ref.dtype)


def _inproj(h, w_in_t, layer, woffs, poffs, dk, tm, tn):
    m, d = 2 * h.shape[0], h.shape[1]
    n = poffs["end"]
    gate_tiles = poffs["q"] // tn
    skip_tiles = (woffs["q"] - woffs["cB"]) // tn
    kern = functools.partial(
        _inproj_kernel,
        sig_end=gate_tiles, silu_lo=poffs["gg"] // tn,
        q_lo=poffs["q"] // tn, q_hi=poffs["k"] // tn, q_scale=float(dk) ** -0.5)
    w_tile = lambda j: jnp.where(j < gate_tiles, j, j + skip_tiles)
    return pl.pallas_call(
        kern,
        grid=(n // tn, m // tm),
        in_specs=[pl.BlockSpec((tm // 2, d), lambda j, i: (i, 0)),
                  pl.BlockSpec((None, tn, d), lambda j, i: (layer, w_tile(j), 0))],
        out_specs=pl.BlockSpec((tm, tn), lambda j, i: (i, j)),
        out_shape=jax.ShapeDtypeStruct((m, n), BF16),
        scratch_shapes=[pltpu.VMEM((d, tn), BF16)],
        compiler_params=_cparams("arbitrary", "arbitrary"),
        name="inproj",
    )(h, w_in_t)


def _gate_kernel(h_ref, wlr_ref, w2_ref, ba_ref, o_ref, *, rank):
    lr = _dot_nt(_unpack_rows(h_ref[...]), wlr_ref[...].astype(BF16))
    lr = jnp.where(lax.broadcasted_iota(jnp.int32, lr.shape, 1) < rank, lr, 0.0)
    logit = _dot(lr.astype(BF16), w2_ref[...]) + ba_ref[...]
    log_sig = jnp.minimum(logit, 0.0) - jnp.log(1.0 + jnp.exp(-jnp.abs(logit)))
    o_ref[...] = log_sig / GLA_TAU


def _gate(h, w_in_t, w2, ba, layer, off_lr, rank):
    m, d = 2 * h.shape[0], h.shape[1]
    rp, dkt = w2.shape[1:]
    tm = _tile(m, 512)
    return pl.pallas_call(
        functools.partial(_gate_kernel, rank=rank),
        grid=(m // tm,),
        in_specs=[pl.BlockSpec((tm // 2, d), lambda i: (i, 0)),
                  pl.BlockSpec((None, rp, d), lambda i: (layer, off_lr // rp, 0)),
                  pl.BlockSpec((None, rp, dkt), lambda i: (layer, 0, 0)),
                  pl.BlockSpec((None, 1, dkt), lambda i: (layer, 0, 0))],
        out_specs=pl.BlockSpec((tm, dkt), lambda i: (i, 0)),
        out_shape=jax.ShapeDtypeStruct((m, dkt), F32),
        compiler_params=_cparams("parallel"),
        name="gate",
    )(h, w_in_t, w2, ba.reshape(ba.shape[0], 1, dkt))


def _block_row(x, period, idx):
    c, w = x.shape
    x3 = x.reshape(c // period, period, w)
    return jnp.broadcast_to(x3[:, idx:idx + 1, :], x3.shape).reshape(c, w)


def _level_factor(q, k, b2, la2, m):
    c, w = b2.shape
    pos = lax.broadcasted_iota(jnp.int32, (c, w), 0) & (2 * m - 1)
    upper = pos >= m
    if 2 * m >= 8:
        e = -jnp.abs(b2 - _block_row(b2, 2 * m, m - 1))
    elif m == 1:
        e = jnp.where(upper, la2, 0.0)
    else:
        assert m == 2
        sub = lax.broadcasted_iota(jnp.int32, (c, w), 0) & 7
        e = -jnp.abs(b2 - jnp.where(sub < 4, _block_row(b2, 8, 1), _block_row(b2, 8, 5)))
    return (jnp.where(upper, q, k) * jnp.exp2(e)).astype(BF16)


def _pair_code(n):
    row = lax.broadcasted_iota(jnp.int32, (n, n), 0)
    col = lax.broadcasted_iota(jnp.int32, (n, n), 1)
    xf = (row ^ col).astype(F32)
    high_bit = (lax.bitcast_convert_type(xf, jnp.int32) >> 23) - 127
    return jnp.where(col < row, high_bit, jnp.where(col == row, -1, -2))


def _diag_scores(q, k, b2, la2, levels, code, blocks):
    qb, kb = q.astype(BF16), k.astype(BF16)
    out = [jnp.where(code == -1, _dot_nt(qb[lo:hi], kb[lo:hi]), 0.0) for lo, hi in blocks]
    for m in levels:
        g = _level_factor(q, k, b2, la2, m)
        bit = m.bit_length() - 1
        out = [jnp.where(code == bit, _dot_nt(g[lo:hi], g[lo:hi]), a) for a, (lo, hi) in zip(out, blocks)]
    return out


def _head_norm_gate(o, gn, gg):
    return _rms(o, gn) * gg


def _levels(c):
    out, m = [], c // 2
    while m >= 1:
        out.append(m)
        m //= 2
    return tuple(out)


def _gla_prompt_body(q_ref, k_ref, v_ref, gg_ref, la_ref, gn_ref, zb_ref, s_ref, tril_ref, code_ref, *, chunk,
                     first_block):
    dk = q_ref.shape[1]
    c, hc = chunk, chunk // 2

    @pl.when(first_block)
    def _():
        s_ref[...] = jnp.zeros_like(s_ref)

    row = lax.broadcasted_iota(jnp.int32, (c, c), 0)
    col = lax.broadcasted_iota(jnp.int32, (c, c), 1)
    tril_ref[...] = (col <= row).astype(BF16)
    code_ref[...] = _pair_code(hc)

    def step(ci, carry):
        rows = pl.ds(pl.multiple_of(ci * c, c), c)
        la = la_ref[rows, :]
        tril = tril_ref[...]
        hi, mid, lo = _split3(la)
        b2 = (_dot(tril, hi) + _dot(tril, mid) + _dot(tril, lo)) * LOG2E
        la2 = la * LOG2E
        b2_last = b2[c - 1:c, :]

        q = q_ref[rows, :].astype(F32)
        k = k_ref[rows, :].astype(F32)
        v = v_ref[rows, :]
        levels = _levels(c)
        a00, a11 = _diag_scores(q, k, b2, la2, levels[1:], code_ref[...], ((0, hc), (hc, c)))
        g_top = _level_factor(q, k, b2, la2, hc)
        a10 = _dot_nt(g_top[hc:], g_top[:hc])
        s_old = s_ref[...]
        o_intra = jnp.concatenate(
            [_dot(a00.astype(BF16), v[:hc]),
             _dot(jnp.concatenate([a10, a11], axis=1).astype(BF16), v)], axis=0)
        o = o_intra + _dot((q * jnp.exp2(b2)).astype(BF16), s_old.astype(BF16))
        packed_rows = pl.ds(pl.multiple_of(ci * hc, hc), hc)
        zb_ref[packed_rows, :] = _pack_rows(_head_norm_gate(o, gn_ref[...], gg_ref[rows, :].astype(F32)))

        kd_t = (k * jnp.exp2(b2_last - b2)).T.astype(BF16)
        eye = lax.broadcasted_iota(jnp.int32, (dk, dk), 0) == lax.broadcasted_iota(jnp.int32, (dk, dk), 1)
        d_col = jnp.sum(jnp.where(eye, jnp.exp2(b2_last), 0.0), axis=1, keepdims=True)
        s_ref[...] = d_col * s_old + _dot(kd_t, v)
        return carry

    lax.fori_loop(0, q_ref.shape[0] // c, step, 0)


def _gla_sample_body(q_ref, k_ref, v_ref, gg_ref, la_ref, gn_ref, sin_ref, zb_ref, sout_ref, o_scr):
    r, dk = q_ref.shape
    nseq = r // SAMPLE_LEN
    la = la_ref[...]
    row = lax.broadcasted_iota(jnp.int32, (r, r), 0)
    col = lax.broadcasted_iota(jnp.int32, (r, r), 1)
    tril = ((col <= row) & ((row ^ col) < SAMPLE_LEN)).astype(BF16)
    hi, mid, lo = _split3(la)
    b2 = (_dot(tril, hi) + _dot(tril, mid) + _dot(tril, lo)) * LOG2E
    b2_last = _block_row(b2, SAMPLE_LEN, SAMPLE_LEN - 1)

    q = q_ref[...].astype(F32)
    k = k_ref[...].astype(F32)
    v = v_ref[...]
    (a,) = _diag_scores(q, k, b2, la * LOG2E, _levels(SAMPLE_LEN), _pair_code(r), ((0, r),))
    o_scr[...] = _dot(a.astype(BF16), v)
    decay = jnp.exp2(b2)
    qe = (q * decay).astype(BF16)
    kd_t = (k * jnp.exp2(b2_last - b2)).T
    d_t = decay.T
    lane_seq = lax.broadcasted_iota(jnp.int32, (dk, r), 1) // SAMPLE_LEN
    for s in range(nseq):
        lo_r, hi_r = s * SAMPLE_LEN, (s + 1) * SAMPLE_LEN
        s_old = sin_ref[s]
        o_scr[lo_r:hi_r, :] += _dot(qe[lo_r:hi_r, :], s_old.astype(BF16))
        kd_s = jnp.where(lane_seq == s, kd_t, 0.0).astype(BF16)
        sout_ref[s] = d_t[:, hi_r - 1:hi_r] * s_old + _dot(kd_s, v)
    zb_ref[...] = _pack_rows(_head_norm_gate(o_scr[...], gn_ref[...], gg_ref[...].astype(F32)))


def _gla_kernel(*refs, chunk, blocks_per_seq, n_alias):
    (qp, kp, vp, ggp, lap, qs, ks, vs, ggs, las, gn, sin), rest = refs[:12], refs[12 + n_alias:]
    zbp, sp, zbs, sout, tril_ref, code_ref, o_scr = rest
    first_block = pl.program_id(0) % blocks_per_seq == 0
    _gla_prompt_body(qp, kp, vp, ggp, lap, gn, zbp, sp, tril_ref, code_ref, chunk=chunk, first_block=first_block)
    _gla_sample_body(qs, ks, vs, ggs, las, gn, sin, zbs, sout, o_scr)


def _gla(p, la, gn, state, stacked, layer, offs, batch, seq, chunk, tblk, sb):
    _, nseq, heads, dk, dv = state.shape
    nb = seq // tblk
    steps = batch * heads * nb
    assert steps == (nseq // sb) * heads
    r = sb * SAMPLE_LEN
    rb0 = batch * seq // r
    qb, kb, vb, gb = offs["q"] // dk, offs["k"] // dk, offs["v"] // dv, offs["gg"] // dv
    prow = lambda s: (s // (heads * nb)) * nb + s % nb
    phead = lambda s: (s // nb) % heads
    srow = lambda s: rb0 + s // heads
    shead = lambda s: s % heads
    st_spec = pl.BlockSpec((None, sb, None, dk, dv), lambda s: (layer, s // heads, shead(s), 0, 0))
    in_specs = [pl.BlockSpec((tblk, dk), lambda s: (prow(s), qb + phead(s))),
                pl.BlockSpec((tblk, dk), lambda s: (prow(s), kb + phead(s))),
                pl.BlockSpec((tblk, dv), lambda s: (prow(s), vb + phead(s))),
                pl.BlockSpec((tblk, dv), lambda s: (prow(s), gb + phead(s))),
                pl.BlockSpec((tblk, dk), lambda s: (prow(s), phead(s))),
                pl.BlockSpec((r, dk), lambda s: (srow(s), qb + shead(s))),
                pl.BlockSpec((r, dk), lambda s: (srow(s), kb + shead(s))),
                pl.BlockSpec((r, dv), lambda s: (srow(s), vb + shead(s))),
                pl.BlockSpec((r, dv), lambda s: (srow(s), gb + shead(s))),
                pl.BlockSpec((r, dk), lambda s: (srow(s), shead(s))),
                pl.BlockSpec((None, 1, dv), lambda s: (layer, 0, 0)),
                st_spec]
    args = [p] * 4 + [la] + [p] * 4 + [la, gn.reshape(gn.shape[0], 1, dv), state]
    aliases = {}
    if stacked is not None:
        for arr, out_idx in zip(stacked, (1, 3)):
            in_specs.append(pl.BlockSpec(memory_space=pl.ANY))
            args.append(arr)
            aliases[len(args) - 1] = out_idx
    return pl.pallas_call(
        functools.partial(_gla_kernel, chunk=chunk, blocks_per_seq=nb, n_alias=len(aliases)),
        grid=(steps,),
        in_specs=in_specs,
        out_specs=[pl.BlockSpec((tblk // 2, dv), lambda s: (prow(s), phead(s))),
                   pl.BlockSpec((None, None, None, dk, dv), lambda s: (layer, s // (heads * nb), phead(s), 0, 0)),
                   pl.BlockSpec((r // 2, dv), lambda s: (s // heads, shead(s))),
                   st_spec],
        out_shape=[jax.ShapeDtypeStruct((batch * seq // 2, heads * dv), jnp.uint32),
                   jax.ShapeDtypeStruct((state.shape[0], batch, heads, dk, dv), F32),
                   jax.ShapeDtypeStruct((nseq * SAMPLE_LEN // 2, heads * dv), jnp.uint32),
                   jax.ShapeDtypeStruct(state.shape, F32)],
        scratch_shapes=[pltpu.VMEM((chunk, chunk), BF16), pltpu.VMEM((chunk // 2, chunk // 2), jnp.int32),
                        pltpu.VMEM((r, dv), F32)],
        input_output_aliases=aliases,
        compiler_params=_cparams("arbitrary"),
        name="gla",
    )(*args)


def _short_conv(u, um1, um2, w):
    return um2 * w[0:1, :] + um1 * w[1:2, :] + u * w[2:3, :]


def _inproj_conv_kernel(h_ref, wcb_ref, wcc_ref, wcx_ref, wcg_ref, cw_ref, st_ref, za_ref, nbp_ref, nbs_ref,
                        wb_ref, halo_ref, *, prompt_tiles, tiles_per_seq, sub_m):
    i = pl.program_id(1)
    tc = wcb_ref.shape[0]
    tm = 2 * h_ref.shape[0]

    @pl.when(i == 0)
    def _():
        for g, ref in enumerate((wcb_ref, wcc_ref, wcx_ref, wcg_ref)):
            wb_ref[:, g * tc:(g + 1) * tc] = ref[...].T.astype(BF16)

    w = cw_ref[...]

    def project(r):
        hh = _unpack_rows(h_ref[r * sub_m // 2:(r + 1) * sub_m // 2, :])
        bc = _dot(hh, wb_ref[:, 0:2 * tc])
        xg = _dot(hh, wb_ref[:, 2 * tc:4 * tc])
        cg = xg[:, tc:]
        return bc[:, :tc] * (cg * _sigmoid(cg)), bc[:, tc:] * xg[:, :tc]

    @pl.when(i < prompt_tiles)
    def _():
        @pl.when(i % tiles_per_seq == 0)
        def _():
            halo_ref[...] = jnp.zeros_like(halo_ref)

        for r in range(tm // sub_m):
            gate, u = project(r)
            hr = halo_ref.shape[0]
            prev = halo_ref[...]
            row = lax.broadcasted_iota(jnp.int32, u.shape, 0)
            um1 = jnp.where(row == 0, prev[hr - 1:hr, :], pltpu.roll(u, 1, 0))
            um2 = jnp.where(row == 0, prev[hr - 2:hr - 1, :],
                            jnp.where(row == 1, prev[hr - 1:hr, :], pltpu.roll(u, 2, 0)))
            za_ref[r * sub_m // 2:(r + 1) * sub_m // 2, :] = _pack_rows(gate * _short_conv(u, um1, um2, w))
            halo_ref[...] = u[sub_m - hr:sub_m, :]
        nbp_ref[...] = u[sub_m - (CONV_WIDTH - 1):sub_m, :]

    @pl.when(i >= prompt_tiles)
    def _():
        nsq = sub_m // SAMPLE_LEN
        for r in range(tm // sub_m):
            gate, u = project(r)
            st = st_ref[r * nsq:(r + 1) * nsq]
            rows = lambda x: jnp.broadcast_to(x, (nsq, SAMPLE_LEN, tc)).reshape(sub_m, tc)
            s0, s1 = rows(st[:, 0:1, :]), rows(st[:, 1:2, :])
            pos = lax.broadcasted_iota(jnp.int32, u.shape, 0) & (SAMPLE_LEN - 1)
            um1 = jnp.where(pos == 0, s1, pltpu.roll(u, 1, 0))
            um2 = jnp.where(pos == 0, s0, jnp.where(pos == 1, s1, pltpu.roll(u, 2, 0)))
            za_ref[r * sub_m // 2:(r + 1) * sub_m // 2, :] = _pack_rows(gate * _short_conv(u, um1, um2, w))
            nbs_ref[r * nsq:(r + 1) * nsq] = u.reshape(nsq, SAMPLE_LEN, tc)[:, SAMPLE_LEN - (CONV_WIDTH - 1):, :]


def _inproj_conv(h, w_in_t, conv_w, state, layer, woffs, batch, seq, tm, tc):
    m, d = 2 * h.shape[0], h.shape[1]
    nsamp, cd = state.shape[1], state.shape[3]
    pt = batch * seq // tm
    tiles_per_seq = seq // tm
    sps = tm // SAMPLE_LEN
    sub_m = min(tm, 512)
    kern = functools.partial(_inproj_conv_kernel, prompt_tiles=pt, tiles_per_seq=tiles_per_seq, sub_m=sub_m)
    w_spec = lambda name: pl.BlockSpec((None, tc, d), lambda j, i: (layer, woffs[name] // tc + j, 0))
    samp = lambda i: jnp.maximum(i - pt, 0)
    return pl.pallas_call(
        kern,
        grid=(cd // tc, m // tm),
        in_specs=[pl.BlockSpec((tm // 2, d), lambda j, i: (i, 0)),
                  w_spec("cB"), w_spec("cC"), w_spec("cx"), w_spec("cg"),
                  pl.BlockSpec((None, CONV_WIDTH, tc), lambda j, i: (layer, 0, j)),
                  pl.BlockSpec((None, sps, CONV_WIDTH - 1, tc), lambda j, i: (layer, samp(i), 0, j))],
        out_specs=[pl.BlockSpec((tm // 2, tc), lambda j, i: (i, j)),
                   pl.BlockSpec((None, CONV_WIDTH - 1, tc),
                                lambda j, i: (jnp.minimum(i // tiles_per_seq, batch - 1), 0, j)),
                   pl.BlockSpec((sps, CONV_WIDTH - 1, tc), lambda j, i: (samp(i), 0, j))],
        out_shape=[jax.ShapeDtypeStruct((m // 2, cd), jnp.uint32),
                   jax.ShapeDtypeStruct((batch, CONV_WIDTH - 1, cd), F32),
                   jax.ShapeDtypeStruct((nsamp, CONV_WIDTH - 1, cd), F32)],
        scratch_shapes=[pltpu.VMEM((d, 4 * tc), BF16), pltpu.VMEM((8, tc), F32)],
        compiler_params=_cparams("arbitrary", "arbitrary"),
        name="inproj_conv",
    )(h, w_in_t, w_in_t, w_in_t, w_in_t, conv_w, state)


def _branch_kernel(za_ref, zbp_ref, zbs_ref, wa_ref, wb_ref, sga_ref, sgb_ref, o_ref, wa_bf, wb_bf, *, prompt_tiles):
    i = pl.program_id(1)

    @pl.when(i == 0)
    def _():
        wa_bf[...] = wa_ref[...].astype(BF16)
        wb_bf[...] = wb_ref[...].astype(BF16)

    def merge(zb_ref):
        tn = o_ref.shape[1]
        sub = min(tn, MXU_WIDTH)
        for c in range(tn // sub):
            cols = slice(c * sub, (c + 1) * sub)
            a = _dot(_unpack_rows(za_ref[...]), wa_bf[:, cols])
            b = _dot(_unpack_rows(zb_ref[...]), wb_bf[:, cols])
            o_ref[:, cols] = _pack_rows(sga_ref[:, cols].astype(F32) * a + sgb_ref[:, cols].astype(F32) * b)

    @pl.when(i < prompt_tiles)
    def _():
        merge(zbp_ref)

    @pl.when(i >= prompt_tiles)
    def _():
        merge(zbs_ref)


def _branch(za, zb_p, zb_s, w_a, w_b, p, layer, offs, tm, tn):
    m, cd = 2 * za.shape[0], za.shape[1]
    dvt = zb_p.shape[1]
    d = w_a.shape[2]
    pt = 2 * zb_p.shape[0] // tm
    gao, gbo = offs["ga"] // tn, offs["gb"] // tn
    return pl.pallas_call(
        functools.partial(_branch_kernel, prompt_tiles=pt),
        grid=(d // tn, m // tm),
        in_specs=[pl.BlockSpec((tm // 2, cd), lambda j, i: (i, 0)),
                  pl.BlockSpec((tm // 2, dvt), lambda j, i: (jnp.minimum(i, pt - 1), 0)),
                  pl.BlockSpec((tm // 2, dvt), lambda j, i: (jnp.maximum(i - pt, 0), 0)),
                  pl.BlockSpec((None, cd, tn), lambda j, i: (layer, 0, j)),
                  pl.BlockSpec((None, dvt, tn), lambda j, i: (layer, 0, j)),
                  pl.BlockSpec((tm, tn), lambda j, i: (i, gao + j)),
                  pl.BlockSpec((tm, tn), lambda j, i: (i, gbo + j))],
        out_specs=pl.BlockSpec((tm // 2, tn), lambda j, i: (i, j)),
        out_shape=jax.ShapeDtypeStruct((m // 2, d), jnp.uint32),
        scratch_shapes=[pltpu.VMEM((cd, tn), BF16), pltpu.VMEM((dvt, tn), BF16)],
        compiler_params=_cparams("arbitrary", "arbitrary"),
        name="branch",
    )(za, zb_p, zb_s, w_a, w_b, p, p)


def _out_kernel(m_ref, w_ref, x_ref, g_ref, xo_ref, ho_ref, w_bf):
    @pl.when(pl.program_id(0) == 0)
    def _():
        w_bf[...] = w_ref[...].astype(BF16)

    x = x_ref[...] + _dot(_unpack_rows(m_ref[...]), w_bf[...])
    xo_ref[...] = x
    ho_ref[...] = _pack_rows(_rms(x, g_ref[...]))


def _out_final_kernel(m_ref, w_ref, x_ref, g_ref, yp_ref, ys_ref, w_bf, *, prompt_tiles):
    i = pl.program_id(0)

    @pl.when(i == 0)
    def _():
        w_bf[...] = w_ref[...].astype(BF16)

    y = _rms(x_ref[...] + _dot(_unpack_rows(m_ref[...]), w_bf[...]), g_ref[...])

    @pl.when(i < prompt_tiles)
    def _():
        yp_ref[...] = y

    @pl.when(i >= prompt_tiles)
    def _():
        ys_ref[...] = y


def _out(merged, w_out, x, g, layer, tm, prompt_rows=None):
    m, d = x.shape
    in_specs = [pl.BlockSpec((tm // 2, d), lambda i: (i, 0)),
                pl.BlockSpec((None, d, d), lambda i: (layer, 0, 0), pipeline_mode=pl.Buffered(1)),
                pl.BlockSpec((tm, d), lambda i: (i, 0)),
                pl.BlockSpec((1, d), lambda i: (0, 0))]
    if prompt_rows is None:
        kern = _out_kernel
        out_specs = [pl.BlockSpec((tm, d), lambda i: (i, 0)), pl.BlockSpec((tm // 2, d), lambda i: (i, 0))]
        out_shape = [jax.ShapeDtypeStruct((m, d), F32), jax.ShapeDtypeStruct((m // 2, d), jnp.uint32)]
    else:
        pt = prompt_rows // tm
        kern = functools.partial(_out_final_kernel, prompt_tiles=pt)
        out_specs = [pl.BlockSpec((tm, d), lambda i: (jnp.minimum(i, pt - 1), 0)),
                     pl.BlockSpec((tm, d), lambda i: (jnp.maximum(i - pt, 0), 0))]
        out_shape = [jax.ShapeDtypeStruct((prompt_rows, d), F32), jax.ShapeDtypeStruct((m - prompt_rows, d), F32)]
    return pl.pallas_call(
        kern,
        grid=(m // tm,),
        in_specs=in_specs,
        out_specs=out_specs,
        out_shape=out_shape,
        scratch_shapes=[pltpu.VMEM((d, d), BF16)],
        compiler_params=_cparams("arbitrary"),
        name="outproj",
    )(merged, w_out, x, g.reshape(1, d))


def kernel(x_prompt, x_sample, state_conv, state_gla, norm_g, w_in, conv_w, w_alpha2, b_alpha,
           gla_norm_g, w_branch_a, w_branch_b, w_out, final_norm_g):
    batch, seq, d = x_prompt.shape
    nsamp, slen, _ = x_sample.shape
    depth, _, _, cd = state_conv.shape
    _, _, heads, dk, dv = state_gla.shape
    rank, dkt = w_alpha2.shape[1:]
    dvt = heads * dv
    assert slen == SAMPLE_LEN and dkt == heads * dk and state_conv.shape[2] == CONV_WIDTH - 1
    widths = (("ga", d), ("gb", d), ("cB", cd), ("cC", cd), ("cx", cd), ("cg", cd),
              ("q", dkt), ("k", dkt), ("v", dvt), ("gg", dvt), ("lr", rank))
    woffs, o = {}, 0
    for name, wd in widths:
        woffs[name] = o
        o += wd
    assert o == w_in.shape[2] and rank <= LANES and woffs["lr"] % LANES == 0 and cd == d
    poffs, o = {}, 0
    for name, wd in widths:
        if name in ("ga", "gb", "q", "k", "v", "gg"):
            poffs[name] = o
            o += wd
    poffs["end"] = o

    mp, ms = batch * seq, nsamp * slen
    m = mp + ms
    tm = _tile(ms, 1024)
    tn = _tile(dkt, 1024)
    tm_wide = next((t for t in (1536, 1024) if m % t == 0), tm)
    chunk = _tile(seq, 256)
    sb = _tile(nsamp, 16)
    tblk = seq * batch * sb // nsamp
    assert seq % tblk == 0 and tblk % chunk == 0
    w2 = jnp.pad(w_alpha2, ((0, 0), (0, LANES - rank), (0, 0))).astype(BF16)
    w_in_t = jnp.swapaxes(w_in, 1, 2)

    x, h = _embed(x_prompt.reshape(mp, d), x_sample.reshape(ms, d), norm_g[0])
    conv_p, conv_s, gla_states = [], [], None
    for l in range(depth):
        p = _inproj(h, w_in_t, l, woffs, poffs, dk, tm_wide, tn)
        za, nb_p, nb_s = _inproj_conv(h, w_in_t, conv_w, state_conv, l, woffs, batch, seq, tm, _tile(cd, MXU_WIDTH))
        la = _gate(h, w_in_t, w2, b_alpha, l, woffs["lr"], rank)
        zb_p, gla_p, zb_s, gla_s = _gla(p, la, gla_norm_g, state_gla, gla_states, l, poffs, batch, seq, chunk,
                                        tblk, sb)
        gla_states = (gla_p, gla_s)
        merged = _branch(za, zb_p, zb_s, w_branch_a, w_branch_b, p, l, poffs, tm, _tile(d, 512))
        conv_p.append(nb_p)
        conv_s.append(nb_s)
        if l < depth - 1:
            x, h = _out(merged, w_out, x, norm_g[l + 1], l, _tile(ms, 512))
        else:
            y_p, y_s = _out(merged, w_out, x, final_norm_g, l, _tile(ms, 512), prompt_rows=mp)
    return (y_p.reshape(batch, seq, d), y_s.reshape(nsamp, slen, d),
            jnp.stack(conv_p), gla_p, jnp.stack(conv_s), gla_s)
```

```python
import functools

import jax
import jax.numpy as jnp
from jax import lax
from jax.experimental import pallas as pl
from jax.experimental.pallas import tpu as pltpu

F32 = jnp.float32
BF16 = jnp.bfloat16
NORM_EPS = 1e-6
GLA_TAU = 16.0
LOG2E = 1.4426950408889634
CONV_WIDTH = 3
SAMPLE_LEN = 8
LANES = 128
MXU_WIDTH = 256
VMEM_LIMIT = 58 << 20


def _cparams(*sem):
    return pltpu.CompilerParams(dimension_semantics=sem, vmem_limit_bytes=VMEM_LIMIT)


def _dot(a, b):
    return jnp.dot(a, b, preferred_element_type=F32)


def _dot_nt(a, b):
    return lax.dot_general(a, b, (((1,), (1,)), ((), ())), preferred_element_type=F32)


def _split3(x):
    hi = x.astype(BF16)
    r1 = x - hi.astype(F32)
    mid = r1.astype(BF16)
    lo = (r1 - mid.astype(F32)).astype(BF16)
    return hi, mid, lo


def _tile(n, pref):
    t = min(n, pref)
    while n % t:
        t //= 2
    return t


def _pack_rows(x):
    return pltpu.bitcast(x.astype(BF16), jnp.uint32)


def _unpack_rows(x):
    return pltpu.bitcast(x, BF16)


def _sigmoid(x):
    return 0.5 * jnp.tanh(0.5 * x) + 0.5


def _rms(x, g):
    ms = jnp.mean(x * x, axis=-1, keepdims=True)
    return x * lax.rsqrt(ms + NORM_EPS) * g


def _embed_kernel(xp_ref, xs_ref, g_ref, x_ref, h_ref, *, prompt_tiles):
    i = pl.program_id(0)

    def emit(x):
        x_ref[...] = x
        h_ref[...] = _pack_rows(_rms(x, g_ref[...]))

    @pl.when(i < prompt_tiles)
    def _():
        emit(xp_ref[...])

    @pl.when(i >= prompt_tiles)
    def _():
        emit(xs_ref[...])


def _embed(xp, xs, g):
    (mp, d), ms = xp.shape, xs.shape[0]
    tm = _tile(ms, 512)
    pt = mp // tm
    kern = functools.partial(_embed_kernel, prompt_tiles=pt)
    return pl.pallas_call(
        kern,
        grid=((mp + ms) // tm,),
        in_specs=[pl.BlockSpec((tm, d), lambda i: (jnp.minimum(i, pt - 1), 0)),
                  pl.BlockSpec((tm, d), lambda i: (jnp.maximum(i - pt, 0), 0)),
                  pl.BlockSpec((1, d), lambda i: (0, 0))],
        out_specs=[pl.BlockSpec((tm, d), lambda i: (i, 0)),
                   pl.BlockSpec((tm // 2, d), lambda i: (i, 0))],
        out_shape=[jax.ShapeDtypeStruct((mp + ms, d), F32),
                   jax.ShapeDtypeStruct(((mp + ms) // 2, d), jnp.uint32)],
        compiler_params=_cparams("arbitrary"),
        name="embed",
    )(xp, xs, g.reshape(1, d))


def _inproj_kernel(h_ref, w_ref, o_ref, wb_ref, *, sig_end, silu_lo, q_lo, q_hi, q_scale):
    j = pl.program_id(0)

    @pl.when(pl.program_id(1) == 0)
    def _():
        wb_ref[...] = w_ref[...].T.astype(BF16)

    is_sig = j < sig_end
    is_silu = j >= silu_lo
    is_q = (j >= q_lo) & (j < q_hi)
    gated = is_sig | is_silu
    alpha = jnp.where(gated, 0.5, jnp.where(is_q, q_scale, 1.0)).astype(F32)
    beta = jnp.where(gated, 0.5, 0.0).astype(F32)
    tm, tn = o_ref.shape
    sub_n = min(tn, 2 * MXU_WIDTH)
    sub_m = next((s for s in (768, 1024, 512) if tm % s == 0), tm)
    for c in range(tn // sub_n):
        cols = slice(c * sub_n, (c + 1) * sub_n)
        for r in range(tm // sub_m):
            acc = _dot(_unpack_rows(h_ref[r * sub_m // 2:(r + 1) * sub_m // 2, :]), wb_ref[:, cols])
            gate = alpha + beta * jnp.tanh(0.5 * acc)
            o_ref[r * sub_m:(r + 1) * sub_m, cols] = (jnp.where(is_sig, 1.0, acc) * gate).astype(o_ref.dtype)


def _inproj(h, w_in_t, layer, woffs, poffs, dk, tm, tn):
    m, d = 2 * h.shape[0], h.shape[1]
    n = poffs["end"]
    gate_tiles = poffs["q"] // tn
    skip_tiles = (woffs["q"] - woffs["cB"]) // tn
    kern = functools.partial(
        _inproj_kernel,
        sig_end=gate_tiles, silu_lo=poffs["gg"] // tn,
        q_lo=poffs["q"] // tn, q_hi=poffs["k"] // tn, q_scale=float(dk) ** -0.5)
    w_tile = lambda j: jnp.where(j < gate_tiles, j, j + skip_tiles)
    return pl.pallas_call(
        kern,
        grid=(n // tn, m // tm),
        in_specs=[pl.BlockSpec((tm // 2, d), lambda j, i: (i, 0)),
                  pl.BlockSpec((None, tn, d), lambda j, i: (layer, w_tile(j), 0))],
        out_specs=pl.BlockSpec((tm, tn), lambda j, i: (i, j)),
        out_shape=jax.ShapeDtypeStruct((m, n), BF16),
        scratch_shapes=[pltpu.VMEM((d, tn), BF16)],
        compiler_params=_cparams("arbitrary", "arbitrary"),
        name="inproj",
    )(h, w_in_t)


def _gate_kernel(h_ref, wlr_ref, w2_ref, ba_ref, o_ref, *, rank):
    lr = _dot_nt(_unpack_rows(h_ref[...]), wlr_ref[...].astype(BF16))
    lr = jnp.where(lax.broadcasted_iota(jnp.int32, lr.shape, 1) < rank, lr, 0.0)
    logit = _dot(lr.astype(BF16), w2_ref[...]) + ba_ref[...]
    log_sig = jnp.minimum(logit, 0.0) - jnp.log(1.0 + jnp.exp(-jnp.abs(logit)))
    o_ref[...] = log_sig / GLA_TAU


def _gate(h, w_in_t, w2, ba, layer, off_lr, rank):
    m, d = 2 * h.shape[0], h.shape[1]
    rp, dkt = w2.shape[1:]
    tm = _tile(m, 1024)
    return pl.pallas_call(
        functools.partial(_gate_kernel, rank=rank),
        grid=(m // tm,),
        in_specs=[pl.BlockSpec((tm // 2, d), lambda i: (i, 0)),
                  pl.BlockSpec((None, rp, d), lambda i: (layer, off_lr // rp, 0)),
                  pl.BlockSpec((None, rp, dkt), lambda i: (layer, 0, 0)),
                  pl.BlockSpec((None, 1, dkt), lambda i: (layer, 0, 0))],
        out_specs=pl.BlockSpec((tm, dkt), lambda i: (i, 0)),
        out_shape=jax.ShapeDtypeStruct((m, dkt), F32),
        compiler_params=_cparams("parallel"),
        name="gate",
    )(h, w_in_t, w2, ba.reshape(ba.shape[0], 1, dkt))


def _block_row(x, period, idx):
    c, w = x.shape
    x3 = x.reshape(c // period, period, w)
    return jnp.broadcast_to(x3[:, idx:idx + 1, :], x3.shape).reshape(c, w)


def _level_factor(q, k, b2, la2, m):
    c, w = b2.shape
    pos = lax.broadcasted_iota(jnp.int32, (c, w), 0) & (2 * m - 1)
    upper = pos >= m
    if 2 * m >= 8:
        e = -jnp.abs(b2 - _block_row(b2, 2 * m, m - 1))
    elif m == 1:
        e = jnp.where(upper, la2, 0.0)
    else:
        assert m == 2
        sub = lax.broadcasted_iota(jnp.int32, (c, w), 0) & 7
        e = -jnp.abs(b2 - jnp.where(sub < 4, _block_row(b2, 8, 1), _block_row(b2, 8, 5)))
    return (jnp.where(upper, q, k) * jnp.exp2(e)).astype(BF16)


def _pair_code(n):
    row = lax.broadcasted_iota(jnp.int32, (n, n), 0)
    col = lax.broadcasted_iota(jnp.int32, (n, n), 1)
    xf = (row ^ col).astype(F32)
    high_bit = (lax.bitcast_convert_type(xf, jnp.int32) >> 23) - 127
    return jnp.where(col < row, high_bit, jnp.where(col == row, -1, -2))


def _diag_scores(q, k, b2, la2, levels, code, blocks):
    qb, kb = q.astype(BF16), k.astype(BF16)
    out = [jnp.where(code == -1, _dot_nt(qb[lo:hi], kb[lo:hi]), 0.0) for lo, hi in blocks]
    for m in levels:
        g = _level_factor(q, k, b2, la2, m)
        bit = m.bit_length() - 1
        out = [jnp.where(code == bit, _dot_nt(g[lo:hi], g[lo:hi]), a) for a, (lo, hi) in zip(out, blocks)]
    return out


def _head_norm_gate(o, gn, gg):
    return _rms(o, gn) * gg


def _levels(c):
    out, m = [], c // 2
    while m >= 1:
        out.append(m)
        m //= 2
    return tuple(out)


def _gla_prompt_body(q_ref, k_ref, v_ref, gg_ref, la_ref, gn_ref, zb_ref, s_ref, tril_ref, code_ref, *, chunk,
                     first_block):
    dk = q_ref.shape[1]
    c, hc = chunk, chunk // 2

    @pl.when(first_block)
    def _():
        s_ref[...] = jnp.zeros_like(s_ref)

    @pl.when(pl.program_id(0) == 0)
    def _():
        row = lax.broadcasted_iota(jnp.int32, (c, c), 0)
        col = lax.broadcasted_iota(jnp.int32, (c, c), 1)
        tril_ref[...] = (col <= row).astype(BF16)
        code_ref[...] = _pair_code(hc)

    def step(ci, carry):
        rows = pl.ds(pl.multiple_of(ci * c, c), c)
        la = la_ref[rows, :]
        tril = tril_ref[...]
        hi, mid, lo = _split3(la)
        b2 = (_dot(tril, hi) + _dot(tril, mid) + _dot(tril, lo)) * LOG2E
        la2 = la * LOG2E
        b2_last = b2[c - 1:c, :]

        q = q_ref[rows, :].astype(F32)
        k = k_ref[rows, :].astype(F32)
        v = v_ref[rows, :]
        levels = _levels(c)
        a00, a11 = _diag_scores(q, k, b2, la2, levels[1:], code_ref[...], ((0, hc), (hc, c)))
        g_top = _level_factor(q, k, b2, la2, hc)
        a10 = _dot_nt(g_top[hc:], g_top[:hc])
        s_old = s_ref[...]
        o_intra = jnp.concatenate(
            [_dot(a00.astype(BF16), v[:hc]),
             _dot(jnp.concatenate([a10, a11], axis=1).astype(BF16), v)], axis=0)
        o = o_intra + _dot((q * jnp.exp2(b2)).astype(BF16), s_old.astype(BF16))
        packed_rows = pl.ds(pl.multiple_of(ci * hc, hc), hc)
        zb_ref[packed_rows, :] = _pack_rows(_head_norm_gate(o, gn_ref[...], gg_ref[rows, :].astype(F32)))

        kd_t = (k * jnp.exp2(b2_last - b2)).T.astype(BF16)
        eye = lax.broadcasted_iota(jnp.int32, (dk, dk), 0) == lax.broadcasted_iota(jnp.int32, (dk, dk), 1)
        d_col = jnp.sum(jnp.where(eye, jnp.exp2(b2_last), 0.0), axis=1, keepdims=True)
        s_ref[...] = d_col * s_old + _dot(kd_t, v)
        return carry

    lax.fori_loop(0, q_ref.shape[0] // c, step, 0)


def _gla_sample_body(q_ref, k_ref, v_ref, gg_ref, la_ref, gn_ref, sin_ref, zb_ref, sout_ref, o_scr):
    r, dk = q_ref.shape
    nseq = r // SAMPLE_LEN
    la = la_ref[...]
    row = lax.broadcasted_iota(jnp.int32, (r, r), 0)
    col = lax.broadcasted_iota(jnp.int32, (r, r), 1)
    tril = ((col <= row) & ((row ^ col) < SAMPLE_LEN)).astype(BF16)
    hi, mid, lo = _split3(la)
    b2 = (_dot(tril, hi) + _dot(tril, mid) + _dot(tril, lo)) * LOG2E
    b2_last = _block_row(b2, SAMPLE_LEN, SAMPLE_LEN - 1)

    q = q_ref[...].astype(F32)
    k = k_ref[...].astype(F32)
    v = v_ref[...]
    (a,) = _diag_scores(q, k, b2, la * LOG2E, _levels(SAMPLE_LEN), _pair_code(r), ((0, r),))
    o_scr[...] = _dot(a.astype(BF16), v)
    decay = jnp.exp2(b2)
    qe = (q * decay).astype(BF16)
    kd_t = (k * jnp.exp2(b2_last - b2)).T
    d_t = decay.T
    lane_seq = lax.broadcasted_iota(jnp.int32, (dk, r), 1) // SAMPLE_LEN
    for s in range(nseq):
        lo_r, hi_r = s * SAMPLE_LEN, (s + 1) * SAMPLE_LEN
        s_old = sin_ref[s]
        o_scr[lo_r:hi_r, :] += _dot(qe[lo_r:hi_r, :], s_old.astype(BF16))
        kd_s = jnp.where(lane_seq == s, kd_t, 0.0).astype(BF16)
        sout_ref[s] = d_t[:, hi_r - 1:hi_r] * s_old + _dot(kd_s, v)
    zb_ref[...] = _pack_rows(_head_norm_gate(o_scr[...], gn_ref[...], gg_ref[...].astype(F32)))


def _gla_kernel(*refs, chunk, blocks_per_seq, n_alias):
    (qp, kp, vp, ggp, lap, qs, ks, vs, ggs, las, gn, sin), rest = refs[:12], refs[12 + n_alias:]
    zbp, sp, zbs, sout, tril_ref, code_ref, o_scr = rest
    first_block = pl.program_id(0) % blocks_per_seq == 0
    _gla_prompt_body(qp, kp, vp, ggp, lap, gn, zbp, sp, tril_ref, code_ref, chunk=chunk, first_block=first_block)
    _gla_sample_body(qs, ks, vs, ggs, las, gn, sin, zbs, sout, o_scr)


def _gla(p, la, gn, state, stacked, layer, offs, batch, seq, chunk, tblk, sb):
    _, nseq, heads, dk, dv = state.shape
    nb = seq // tblk
    steps = batch * heads * nb
    assert steps == (nseq // sb) * heads
    r = sb * SAMPLE_LEN
    rb0 = batch * seq // r
    qb, kb, vb, gb = offs["q"] // dk, offs["k"] // dk, offs["v"] // dv, offs["gg"] // dv
    prow = lambda s: (s // (heads * nb)) * nb + s % nb
    phead = lambda s: (s // nb) % heads
    srow = lambda s: rb0 + s // heads
    shead = lambda s: s % heads
    st_spec = pl.BlockSpec((None, sb, None, dk, dv), lambda s: (layer, s // heads, shead(s), 0, 0))
    in_specs = [pl.BlockSpec((tblk, dk), lambda s: (prow(s), qb + phead(s))),
                pl.BlockSpec((tblk, dk), lambda s: (prow(s), kb + phead(s))),
                pl.BlockSpec((tblk, dv), lambda s: (prow(s), vb + phead(s))),
                pl.BlockSpec((tblk, dv), lambda s: (prow(s), gb + phead(s))),
                pl.BlockSpec((tblk, dk), lambda s: (prow(s), phead(s))),
                pl.BlockSpec((r, dk), lambda s: (srow(s), qb + shead(s))),
                pl.BlockSpec((r, dk), lambda s: (srow(s), kb + shead(s))),
                pl.BlockSpec((r, dv), lambda s: (srow(s), vb + shead(s))),
                pl.BlockSpec((r, dv), lambda s: (srow(s), gb + shead(s))),
                pl.BlockSpec((r, dk), lambda s: (srow(s), shead(s))),
                pl.BlockSpec((None, 1, dv), lambda s: (layer, 0, 0)),
                st_spec]
    args = [p] * 4 + [la] + [p] * 4 + [la, gn.reshape(gn.shape[0], 1, dv), state]
    aliases = {}
    if stacked is not None:
        for arr, out_idx in zip(stacked, (1, 3)):
            in_specs.append(pl.BlockSpec(memory_space=pl.ANY))
            args.append(arr)
            aliases[len(args) - 1] = out_idx
    return pl.pallas_call(
        functools.partial(_gla_kernel, chunk=chunk, blocks_per_seq=nb, n_alias=len(aliases)),
        grid=(steps,),
        in_specs=in_specs,
        out_specs=[pl.BlockSpec((tblk // 2, dv), lambda s: (prow(s), phead(s))),
                   pl.BlockSpec((None, None, None, dk, dv), lambda s: (layer, s // (heads * nb), phead(s), 0, 0)),
                   pl.BlockSpec((r // 2, dv), lambda s: (s // heads, shead(s))),
                   st_spec],
        out_shape=[jax.ShapeDtypeStruct((batch * seq // 2, heads * dv), jnp.uint32),
                   jax.ShapeDtypeStruct((state.shape[0], batch, heads, dk, dv), F32),
                   jax.ShapeDtypeStruct((nseq * SAMPLE_LEN // 2, heads * dv), jnp.uint32),
                   jax.ShapeDtypeStruct(state.shape, F32)],
        scratch_shapes=[pltpu.VMEM((chunk, chunk), BF16), pltpu.VMEM((chunk // 2, chunk // 2), jnp.int32),
                        pltpu.VMEM((r, dv), F32)],
        input_output_aliases=aliases,
        compiler_params=_cparams("arbitrary"),
        name="gla",
    )(*args)


def _short_conv(u, um1, um2, w):
    return um2 * w[0:1, :] + um1 * w[1:2, :] + u * w[2:3, :]


def _inproj_conv_kernel(h_ref, wcb_ref, wcc_ref, wcx_ref, wcg_ref, cw_ref, st_ref, za_ref, nbp_ref, nbs_ref,
                        wb_ref, halo_ref, *, prompt_tiles, tiles_per_seq, sub_m):
    i = pl.program_id(1)
    tc = wcb_ref.shape[0]
    tm = 2 * h_ref.shape[0]

    @pl.when(i == 0)
    def _():
        for g, ref in enumerate((wcb_ref, wcc_ref, wcx_ref, wcg_ref)):
            wb_ref[:, g * tc:(g + 1) * tc] = ref[...].T.astype(BF16)

    w = cw_ref[...]

    def project(r):
        hh = _unpack_rows(h_ref[r * sub_m // 2:(r + 1) * sub_m // 2, :])
        bc = _dot(hh, wb_ref[:, 0:2 * tc])
        xg = _dot(hh, wb_ref[:, 2 * tc:4 * tc])
        cg = xg[:, tc:]
        return bc[:, :tc] * (cg * _sigmoid(cg)), bc[:, tc:] * xg[:, :tc]

    @pl.when(i < prompt_tiles)
    def _():
        @pl.when(i % tiles_per_seq == 0)
        def _():
            halo_ref[...] = jnp.zeros_like(halo_ref)

        for r in range(tm // sub_m):
            gate, u = project(r)
            hr = halo_ref.shape[0]
            prev = halo_ref[...]
            row = lax.broadcasted_iota(jnp.int32, u.shape, 0)
            um1 = jnp.where(row == 0, prev[hr - 1:hr, :], pltpu.roll(u, 1, 0))
            um2 = jnp.where(row == 0, prev[hr - 2:hr - 1, :],
                            jnp.where(row == 1, prev[hr - 1:hr, :], pltpu.roll(u, 2, 0)))
            za_ref[r * sub_m // 2:(r + 1) * sub_m // 2, :] = _pack_rows(gate * _short_conv(u, um1, um2, w))
            halo_ref[...] = u[sub_m - hr:sub_m, :]
        nbp_ref[...] = u[sub_m - (CONV_WIDTH - 1):sub_m, :]

    @pl.when(i >= prompt_tiles)
    def _():
        nsq = sub_m // SAMPLE_LEN
        for r in range(tm // sub_m):
            gate, u = project(r)
            st = st_ref[r * nsq:(r + 1) * nsq]
            rows = lambda x: jnp.broadcast_to(x, (nsq, SAMPLE_LEN, tc)).reshape(sub_m, tc)
            s0, s1 = rows(st[:, 0:1, :]), rows(st[:, 1:2, :])
            pos = lax.broadcasted_iota(jnp.int32, u.shape, 0) & (SAMPLE_LEN - 1)
            um1 = jnp.where(pos == 0, s1, pltpu.roll(u, 1, 0))
            um2 = jnp.where(pos == 0, s0, jnp.where(pos == 1, s1, pltpu.roll(u, 2, 0)))
            za_ref[r * sub_m // 2:(r + 1) * sub_m // 2, :] = _pack_rows(gate * _short_conv(u, um1, um2, w))
            nbs_ref[r * nsq:(r + 1) * nsq] = u.reshape(nsq, SAMPLE_LEN, tc)[:, SAMPLE_LEN - (CONV_WIDTH - 1):, :]


def _inproj_conv(h, w_in_t, conv_w, state, layer, woffs, batch, seq, tm, tc):
    m, d = 2 * h.shape[0], h.shape[1]
    nsamp, cd = state.shape[1], state.shape[3]
    pt = batch * seq // tm
    tiles_per_seq = seq // tm
    sps = tm // SAMPLE_LEN
    sub_m = min(tm, 512)
    kern = functools.partial(_inproj_conv_kernel, prompt_tiles=pt, tiles_per_seq=tiles_per_seq, sub_m=sub_m)
    w_spec = lambda name: pl.BlockSpec((None, tc, d), lambda j, i: (layer, woffs[name] // tc + j, 0))
    samp = lambda i: jnp.maximum(i - pt, 0)
    return pl.pallas_call(
        kern,
        grid=(cd // tc, m // tm),
        in_specs=[pl.BlockSpec((tm // 2, d), lambda j, i: (i, 0)),
                  w_spec("cB"), w_spec("cC"), w_spec("cx"), w_spec("cg"),
                  pl.BlockSpec((None, CONV_WIDTH, tc), lambda j, i: (layer, 0, j)),
                  pl.BlockSpec((None, sps, CONV_WIDTH - 1, tc), lambda j, i: (layer, samp(i), 0, j))],
        out_specs=[pl.BlockSpec((tm // 2, tc), lambda j, i: (i, j)),
                   pl.BlockSpec((None, CONV_WIDTH - 1, tc),
                                lambda j, i: (jnp.minimum(i // tiles_per_seq, batch - 1), 0, j)),
                   pl.BlockSpec((sps, CONV_WIDTH - 1, tc), lambda j, i: (samp(i), 0, j))],
        out_shape=[jax.ShapeDtypeStruct((m // 2, cd), jnp.uint32),
                   jax.ShapeDtypeStruct((batch, CONV_WIDTH - 1, cd), F32),
                   jax.ShapeDtypeStruct((nsamp, CONV_WIDTH - 1, cd), F32)],
        scratch_shapes=[pltpu.VMEM((d, 4 * tc), BF16), pltpu.VMEM((8, tc), F32)],
        compiler_params=_cparams("arbitrary", "arbitrary"),
        name="inproj_conv",
    )(h, w_in_t, w_in_t, w_in_t, w_in_t, conv_w, state)


def _branch_kernel(za_ref, zbp_ref, zbs_ref, wa_ref, wb_ref, sga_ref, sgb_ref, o_ref, wa_bf, wb_bf, *, prompt_tiles):
    i = pl.program_id(1)

    @pl.when(i == 0)
    def _():
        wa_bf[...] = wa_ref[...].astype(BF16)
        wb_bf[...] = wb_ref[...].astype(BF16)

    def merge(zb_ref):
        tn = o_ref.shape[1]
        sub = min(tn, MXU_WIDTH)
        for c in range(tn // sub):
            cols = slice(c * sub, (c + 1) * sub)
            a = _dot(_unpack_rows(za_ref[...]), wa_bf[:, cols])
            b = _dot(_unpack_rows(zb_ref[...]), wb_bf[:, cols])
            o_ref[:, cols] = _pack_rows(sga_ref[:, cols].astype(F32) * a + sgb_ref[:, cols].astype(F32) * b)

    @pl.when(i < prompt_tiles)
    def _():
        merge(zbp_ref)

    @pl.when(i >= prompt_tiles)
    def _():
        merge(zbs_ref)


def _branch(za, zb_p, zb_s, w_a, w_b, p, layer, offs, tm, tn):
    m, cd = 2 * za.shape[0], za.shape[1]
    dvt = zb_p.shape[1]
    d = w_a.shape[2]
    pt = 2 * zb_p.shape[0] // tm
    gao, gbo = offs["ga"] // tn, offs["gb"] // tn
    return pl.pallas_call(
        functools.partial(_branch_kernel, prompt_tiles=pt),
        grid=(d // tn, m // tm),
        in_specs=[pl.BlockSpec((tm // 2, cd), lambda j, i: (i, 0)),
                  pl.BlockSpec((tm // 2, dvt), lambda j, i: (jnp.minimum(i, pt - 1), 0)),
                  pl.BlockSpec((tm // 2, dvt), lambda j, i: (jnp.maximum(i - pt, 0), 0)),
                  pl.BlockSpec((None, cd, tn), lambda j, i: (layer, 0, j)),
                  pl.BlockSpec((None, dvt, tn), lambda j, i: (layer, 0, j)),
                  pl.BlockSpec((tm, tn), lambda j, i: (i, gao + j)),
                  pl.BlockSpec((tm, tn), lambda j, i: (i, gbo + j))],
        out_specs=pl.BlockSpec((tm // 2, tn), lambda j, i: (i, j)),
        out_shape=jax.ShapeDtypeStruct((m // 2, d), jnp.uint32),
        scratch_shapes=[pltpu.VMEM((cd, tn), BF16), pltpu.VMEM((dvt, tn), BF16)],
        compiler_params=_cparams("arbitrary", "arbitrary"),
        name="branch",
    )(za, zb_p, zb_s, w_a, w_b, p, p)


def _out_kernel(m_ref, w_ref, x_ref, g_ref, xo_ref, ho_ref, w_bf):
    @pl.when(pl.program_id(0) == 0)
    def _():
        w_bf[...] = w_ref[...].astype(BF16)

    x = x_ref[...] + _dot(_unpack_rows(m_ref[...]), w_bf[...])
    xo_ref[...] = x
    ho_ref[...] = _pack_rows(_rms(x, g_ref[...]))


def _out_final_kernel(m_ref, w_ref, x_ref, g_ref, yp_ref, ys_ref, w_bf, *, prompt_tiles):
    i = pl.program_id(0)

    @pl.when(i == 0)
    def _():
        w_bf[...] = w_ref[...].astype(BF16)

    y = _rms(x_ref[...] + _dot(_unpack_rows(m_ref[...]), w_bf[...]), g_ref[...])

    @pl.when(i < prompt_tiles)
    def _():
        yp_ref[...] = y

    @pl.when(i >= prompt_tiles)
    def _():
        ys_ref[...] = y


def _out(merged, w_out, x, g, layer, tm, prompt_rows=None):
    m, d = x.shape
    in_specs = [pl.BlockSpec((tm // 2, d), lambda i: (i, 0)),
                pl.BlockSpec((None, d, d), lambda i: (layer, 0, 0), pipeline_mode=pl.Buffered(1)),
                pl.BlockSpec((tm, d), lambda i: (i, 0)),
                pl.BlockSpec((1, d), lambda i: (0, 0))]
    if prompt_rows is None:
        kern = _out_kernel
        out_specs = [pl.BlockSpec((tm, d), lambda i: (i, 0)), pl.BlockSpec((tm // 2, d), lambda i: (i, 0))]
        out_shape = [jax.ShapeDtypeStruct((m, d), F32), jax.ShapeDtypeStruct((m // 2, d), jnp.uint32)]
    else:
        pt = prompt_rows // tm
        kern = functools.partial(_out_final_kernel, prompt_tiles=pt)
        out_specs = [pl.BlockSpec((tm, d), lambda i: (jnp.minimum(i, pt - 1), 0)),
                     pl.BlockSpec((tm, d), lambda i: (jnp.maximum(i - pt, 0), 0))]
        out_shape = [jax.ShapeDtypeStruct((prompt_rows, d), F32), jax.ShapeDtypeStruct((m - prompt_rows, d), F32)]
    return pl.pallas_call(
        kern,
        grid=(m // tm,),
        in_specs=in_specs,
        out_specs=out_specs,
        out_shape=out_shape,
        scratch_shapes=[pltpu.VMEM((d, d), BF16)],
        compiler_params=_cparams("arbitrary"),
        name="outproj",
    )(merged, w_out, x, g.reshape(1, d))


def kernel(x_prompt, x_sample, state_conv, state_gla, norm_g, w_in, conv_w, w_alpha2, b_alpha,
           gla_norm_g, w_branch_a, w_branch_b, w_out, final_norm_g):
    batch, seq, d = x_prompt.shape
    nsamp, slen, _ = x_sample.shape
    depth, _, _, cd = state_conv.shape
    _, _, heads, dk, dv = state_gla.shape
    rank, dkt = w_alpha2.shape[1:]
    dvt = heads * dv
    assert slen == SAMPLE_LEN and dkt == heads * dk and state_conv.shape[2] == CONV_WIDTH - 1
    widths = (("ga", d), ("gb", d), ("cB", cd), ("cC", cd), ("cx", cd), ("cg", cd),
              ("q", dkt), ("k", dkt), ("v", dvt), ("gg", dvt), ("lr", rank))
    woffs, o = {}, 0
    for name, wd in widths:
        woffs[name] = o
        o += wd
    assert o == w_in.shape[2] and rank <= LANES and woffs["lr"] % LANES == 0 and cd == d
    poffs, o = {}, 0
    for name, wd in widths:
        if name in ("ga", "gb", "q", "k", "v", "gg"):
            poffs[name] = o
            o += wd
    poffs["end"] = o

    mp, ms = batch * seq, nsamp * slen
    m = mp + ms
    tm = _tile(ms, 1024)
    tn = _tile(dkt, 1024)
    tm_wide = next((t for t in (1536, 1024) if m % t == 0), tm)
    chunk = _tile(seq, 256)
    sb = _tile(nsamp, 16)
    tblk = seq * batch * sb // nsamp
    assert seq % tblk == 0 and tblk % chunk == 0
    w2 = jnp.pad(w_alpha2, ((0, 0), (0, LANES - rank), (0, 0))).astype(BF16)
    w_in_t = jnp.swapaxes(w_in, 1, 2)

    x, h = _embed(x_prompt.reshape(mp, d), x_sample.reshape(ms, d), norm_g[0])
    conv_p, conv_s, gla_states = [], [], None
    for l in range(depth):
        p = _inproj(h, w_in_t, l, woffs, poffs, dk, tm_wide, tn)
        za, nb_p, nb_s = _inproj_conv(h, w_in_t, conv_w, state_conv, l, woffs, batch, seq, tm, _tile(cd, MXU_WIDTH))
        la = _gate(h, w_in_t, w2, b_alpha, l, woffs["lr"], rank)
        zb_p, gla_p, zb_s, gla_s = _gla(p, la, gla_norm_g, state_gla, gla_states, l, poffs, batch, seq, chunk,
                                        tblk, sb)
        gla_states = (gla_p, gla_s)
        merged = _branch(za, zb_p, zb_s, w_branch_a, w_branch_b, p, l, poffs, tm, _tile(d, 512))
        conv_p.append(nb_p)
        conv_s.append(nb_s)
        if l < depth - 1:
            x, h = _out(merged, w_out, x, norm_g[l + 1], l, _tile(ms, 512))
        else:
            y_p, y_s = _out(merged, w_out, x, final_norm_g, l, _tile(ms, 512), prompt_rows=mp)
    return (y_p.reshape(batch, seq, d), y_s.reshape(nsamp, slen, d),
            jnp.stack(conv_p), gla_p, jnp.stack(conv_s), gla_s)
```

```python
import functools

import jax
import jax.numpy as jnp
from jax import lax
from jax.experimental import pallas as pl
from jax.experimental.pallas import tpu as pltpu

F32 = jnp.float32
BF16 = jnp.bfloat16
NORM_EPS = 1e-6
GLA_TAU = 16.0
LOG2E = 1.4426950408889634
CONV_WIDTH = 3
SAMPLE_LEN = 8
LANES = 128
MXU_WIDTH = 256
VMEM_LIMIT = 58 << 20


def _cparams(*sem):
    return pltpu.CompilerParams(dimension_semantics=sem, vmem_limit_bytes=VMEM_LIMIT)


def _dot(a, b):
    return jnp.dot(a, b, preferred_element_type=F32)


def _dot_nt(a, b):
    return lax.dot_general(a, b, (((1,), (1,)), ((), ())), preferred_element_type=F32)


def _split3(x):
    hi = x.astype(BF16)
    r1 = x - hi.astype(F32)
    mid = r1.astype(BF16)
    lo = (r1 - mid.astype(F32)).astype(BF16)
    return hi, mid, lo


def _tile(n, pref):
    t = min(n, pref)
    while n % t:
        t //= 2
    return t


def _pack_rows(x):
    return pltpu.bitcast(x.astype(BF16), jnp.uint32)


def _unpack_rows(x):
    return pltpu.bitcast(x, BF16)


def _sigmoid(x):
    return 0.5 * jnp.tanh(0.5 * x) + 0.5


def _rms(x, g):
    ms = jnp.mean(x * x, axis=-1, keepdims=True)
    return x * lax.rsqrt(ms + NORM_EPS) * g


def _embed_kernel(xp_ref, xs_ref, g_ref, x_ref, h_ref, *, prompt_tiles):
    i = pl.program_id(0)

    def emit(x):
        x_ref[...] = x
        h_ref[...] = _pack_rows(_rms(x, g_ref[...]))

    @pl.when(i < prompt_tiles)
    def _():
        emit(xp_ref[...])

    @pl.when(i >= prompt_tiles)
    def _():
        emit(xs_ref[...])


def _embed(xp, xs, g):
    (mp, d), ms = xp.shape, xs.shape[0]
    tm = _tile(ms, 512)
    pt = mp // tm
    kern = functools.partial(_embed_kernel, prompt_tiles=pt)
    return pl.pallas_call(
        kern,
        grid=((mp + ms) // tm,),
        in_specs=[pl.BlockSpec((tm, d), lambda i: (jnp.minimum(i, pt - 1), 0)),
                  pl.BlockSpec((tm, d), lambda i: (jnp.maximum(i - pt, 0), 0)),
                  pl.BlockSpec((1, d), lambda i: (0, 0))],
        out_specs=[pl.BlockSpec((tm, d), lambda i: (i, 0)),
                   pl.BlockSpec((tm // 2, d), lambda i: (i, 0))],
        out_shape=[jax.ShapeDtypeStruct((mp + ms, d), F32),
                   jax.ShapeDtypeStruct(((mp + ms) // 2, d), jnp.uint32)],
        compiler_params=_cparams("arbitrary"),
        name="embed",
    )(xp, xs, g.reshape(1, d))


def _inproj_kernel(h_ref, w_ref, o_ref, wb_ref, *, sig_end, silu_lo, q_lo, q_hi, q_scale):
    j = pl.program_id(0)

    @pl.when(pl.program_id(1) == 0)
    def _():
        wb_ref[...] = w_ref[...].T.astype(BF16)

    is_sig = j < sig_end
    is_silu = j >= silu_lo
    is_q = (j >= q_lo) & (j < q_hi)
    gated = is_sig | is_silu
    alpha = jnp.where(gated, 0.5, jnp.where(is_q, q_scale, 1.0)).astype(F32)
    beta = jnp.where(gated, 0.5, 0.0).astype(F32)
    tm, tn = o_ref.shape
    sub_n = min(tn, 2 * MXU_WIDTH)
    sub_m = next((s for s in (768, 1024, 512) if tm % s == 0), tm)
    for c in range(tn // sub_n):
        cols = slice(c * sub_n, (c + 1) * sub_n)
        for r in range(tm // sub_m):
            acc = _dot(_unpack_rows(h_ref[r * sub_m // 2:(r + 1) * sub_m // 2, :]), wb_ref[:, cols])
            gate = alpha + beta * jnp.tanh(0.5 * acc)
            o_ref[r * sub_m:(r + 1) * sub_m, cols] = (jnp.where(is_sig, 1.0, acc) * gate).astype(o_ref.dtype)


def _inproj(h, w_in_t, layer, woffs, poffs, dk, tm, tn):
    m, d = 2 * h.shape[0], h.shape[1]
    n = poffs["end"]
    gate_tiles = poffs["q"] // tn
    skip_tiles = (woffs["q"] - woffs["cB"]) // tn
    kern = functools.partial(
        _inproj_kernel,
        sig_end=gate_tiles, silu_lo=poffs["gg"] // tn,
        q_lo=poffs["q"] // tn, q_hi=poffs["k"] // tn, q_scale=float(dk) ** -0.5)
    w_tile = lambda j: jnp.where(j < gate_tiles, j, j + skip_tiles)
    return pl.pallas_call(
        kern,
        grid=(n // tn, m // tm),
        in_specs=[pl.BlockSpec((tm // 2, d), lambda j, i: (i, 0)),
                  pl.BlockSpec((None, tn, d), lambda j, i: (layer, w_tile(j), 0))],
        out_specs=pl.BlockSpec((tm, tn), lambda j, i: (i, j)),
        out_shape=jax.ShapeDtypeStruct((m, n), BF16),
        scratch_shapes=[pltpu.VMEM((d, tn), BF16)],
        compiler_params=_cparams("arbitrary", "arbitrary"),
        name="inproj",
    )(h, w_in_t)


def _gate_kernel(h_ref, wlr_ref, w2_ref, ba_ref, o_ref, *, rank):
    lr = _dot_nt(_unpack_rows(h_ref[...]), wlr_ref[...].astype(BF16))
    lr = jnp.where(lax.broadcasted_iota(jnp.int32, lr.shape, 1) < rank, lr, 0.0)
    logit = _dot(lr.astype(BF16), w2_ref[...]) + ba_ref[...]
    log_sig = jnp.minimum(logit, 0.0) - jnp.log(1.0 + jnp.exp(-jnp.abs(logit)))
    o_ref[...] = log_sig / GLA_TAU


def _gate(h, w_in_t, w2, ba, layer, off_lr, rank):
    m, d = 2 * h.shape[0], h.shape[1]
    rp, dkt = w2.shape[1:]
    tm = _tile(m, 1024)
    return pl.pallas_call(
        functools.partial(_gate_kernel, rank=rank),
        grid=(m // tm,),
        in_specs=[pl.BlockSpec((tm // 2, d), lambda i: (i, 0)),
                  pl.BlockSpec((None, rp, d), lambda i: (layer, off_lr // rp, 0)),
                  pl.BlockSpec((None, rp, dkt), lambda i: (layer, 0, 0)),
                  pl.BlockSpec((None, 1, dkt), lambda i: (layer, 0, 0))],
        out_specs=pl.BlockSpec((tm, dkt), lambda i: (i, 0)),
        out_shape=jax.ShapeDtypeStruct((m, dkt), F32),
        compiler_params=_cparams("parallel"),
        name="gate",
    )(h, w_in_t, w2, ba.reshape(ba.shape[0], 1, dkt))


def _block_row(x, period, idx):
    c, w = x.shape
    x3 = x.reshape(c // period, period, w)
    return jnp.broadcast_to(x3[:, idx:idx + 1, :], x3.shape).reshape(c, w)


def _level_factor(q, k, b2, la2, m):
    c, w = b2.shape
    pos = lax.broadcasted_iota(jnp.int32, (c, w), 0) & (2 * m - 1)
    upper = pos >= m
    if 2 * m >= 8:
        e = -jnp.abs(b2 - _block_row(b2, 2 * m, m - 1))
    elif m == 1:
        e = jnp.where(upper, la2, 0.0)
    else:
        assert m == 2
        sub = lax.broadcasted_iota(jnp.int32, (c, w), 0) & 7
        e = -jnp.abs(b2 - jnp.where(sub < 4, _block_row(b2, 8, 1), _block_row(b2, 8, 5)))
    return (jnp.where(upper, q, k) * jnp.exp2(e)).astype(BF16)


def _pair_code(n):
    row = lax.broadcasted_iota(jnp.int32, (n, n), 0)
    col = lax.broadcasted_iota(jnp.int32, (n, n), 1)
    xf = (row ^ col).astype(F32)
    high_bit = (lax.bitcast_convert_type(xf, jnp.int32) >> 23) - 127
    return jnp.where(col < row, high_bit, jnp.where(col == row, -1, -2))


def _diag_scores(q, k, b2, la2, levels, code, blocks):
    qb, kb = q.astype(BF16), k.astype(BF16)
    out = [jnp.where(code == -1, _dot_nt(qb[lo:hi], kb[lo:hi]), 0.0) for lo, hi in blocks]
    for m in levels:
        g = _level_factor(q, k, b2, la2, m)
        bit = m.bit_length() - 1
        out = [jnp.where(code == bit, _dot_nt(g[lo:hi], g[lo:hi]), a) for a, (lo, hi) in zip(out, blocks)]
    return out


def _head_norm_gate(o, gn, gg):
    return _rms(o, gn) * gg


def _levels(c):
    out, m = [], c // 2
    while m >= 1:
        out.append(m)
        m //= 2
    return tuple(out)


GLA_ROW_BLOCK = 64


def _gla_prompt_body(q_ref, k_ref, v_ref, gg_ref, la_ref, gn_ref, zb_ref, s_ref, tril_ref, code_ref,
                     b2_scr, g_scr, qe_scr, kd_scr, *, chunk, first_block):
    dk = q_ref.shape[1]
    c, hc = chunk, chunk // 2
    levels = _levels(c)
    rb = min(GLA_ROW_BLOCK, c)

    @pl.when(first_block)
    def _():
        s_ref[...] = jnp.zeros_like(s_ref)

    @pl.when(pl.program_id(0) == 0)
    def _():
        row = lax.broadcasted_iota(jnp.int32, (c, c), 0)
        col = lax.broadcasted_iota(jnp.int32, (c, c), 1)
        tril_ref[...] = (col <= row).astype(BF16)
        code_ref[...] = _pair_code(hc)

    def step(ci, carry):
        base = pl.multiple_of(ci * c, c)
        rows = pl.ds(base, c)
        tril = tril_ref[...]
        hi, mid, lo = _split3(la_ref[rows, :])
        b2_scr[...] = (_dot(tril, hi) + _dot(tril, mid) + _dot(tril, lo)) * LOG2E
        b2_last = b2_scr[c - 1:c, :]

        for r0 in range(0, c, rb):
            blk = pl.ds(base + r0, rb)
            q = q_ref[blk, :].astype(F32)
            k = k_ref[blk, :].astype(F32)
            b2 = b2_scr[r0:r0 + rb, :]
            la2 = la_ref[blk, :] * LOG2E
            for li, m in enumerate(levels):
                if 2 * m <= rb:
                    g = _level_factor(q, k, b2, la2, m)
                else:
                    mid_row = (r0 // (2 * m)) * 2 * m + m - 1
                    src = q if (r0 // m) % 2 else k
                    g = (src * jnp.exp2(-jnp.abs(b2 - b2_scr[mid_row:mid_row + 1, :]))).astype(BF16)
                g_scr[li, r0:r0 + rb, :] = g
            qe_scr[r0:r0 + rb, :] = (q * jnp.exp2(b2)).astype(BF16)
            kd_scr[r0:r0 + rb, :] = k * jnp.exp2(b2_last - b2)

        v = v_ref[rows, :]
        code = code_ref[...]
        halves = []
        for lo_r, hi_r in ((0, hc), (hc, c)):
            a = jnp.where(code == -1, _dot_nt(q_ref[pl.ds(base + lo_r, hc), :], k_ref[pl.ds(base + lo_r, hc), :]), 0.0)
            for li, m in enumerate(levels):
                if m < hc:
                    g = g_scr[li, lo_r:hi_r, :]
                    a = jnp.where(code == m.bit_length() - 1, _dot_nt(g, g), a)
            halves.append(a)
        a00, a11 = halves
        a10 = _dot_nt(g_scr[0, hc:, :], g_scr[0, :hc, :])
        s_old = s_ref[...]
        o_intra = jnp.concatenate(
            [_dot(a00.astype(BF16), v[:hc]),
             _dot(jnp.concatenate([a10, a11], axis=1).astype(BF16), v)], axis=0)
        o = o_intra + _dot(qe_scr[...], s_old.astype(BF16))
        packed_rows = pl.ds(pl.multiple_of(ci * hc, hc), hc)
        zb_ref[packed_rows, :] = _pack_rows(_head_norm_gate(o, gn_ref[...], gg_ref[rows, :].astype(F32)))

        kd_t = kd_scr[...].T.astype(BF16)
        eye = lax.broadcasted_iota(jnp.int32, (dk, dk), 0) == lax.broadcasted_iota(jnp.int32, (dk, dk), 1)
        d_col = jnp.sum(jnp.where(eye, jnp.exp2(b2_last), 0.0), axis=1, keepdims=True)
        s_ref[...] = d_col * s_old + _dot(kd_t, v)
        return carry

    lax.fori_loop(0, q_ref.shape[0] // c, step, 0)


def _gla_sample_body(q_ref, k_ref, v_ref, gg_ref, la_ref, gn_ref, sin_ref, zb_ref, sout_ref, o_scr):
    r, dk = q_ref.shape
    nseq = r // SAMPLE_LEN
    la = la_ref[...]
    row = lax.broadcasted_iota(jnp.int32, (r, r), 0)
    col = lax.broadcasted_iota(jnp.int32, (r, r), 1)
    tril = ((col <= row) & ((row ^ col) < SAMPLE_LEN)).astype(BF16)
    hi, mid, lo = _split3(la)
    b2 = (_dot(tril, hi) + _dot(tril, mid) + _dot(tril, lo)) * LOG2E
    b2_last = _block_row(b2, SAMPLE_LEN, SAMPLE_LEN - 1)

    q = q_ref[...].astype(F32)
    k = k_ref[...].astype(F32)
    v = v_ref[...]
    (a,) = _diag_scores(q, k, b2, la * LOG2E, _levels(SAMPLE_LEN), _pair_code(r), ((0, r),))
    o_scr[...] = _dot(a.astype(BF16), v)
    decay = jnp.exp2(b2)
    qe = (q * decay).astype(BF16)
    kd_t = (k * jnp.exp2(b2_last - b2)).T
    d_t = decay.T
    lane_seq = lax.broadcasted_iota(jnp.int32, (dk, r), 1) // SAMPLE_LEN
    for s in range(nseq):
        lo_r, hi_r = s * SAMPLE_LEN, (s + 1) * SAMPLE_LEN
        s_old = sin_ref[s]
        o_scr[lo_r:hi_r, :] += _dot(qe[lo_r:hi_r, :], s_old.astype(BF16))
        kd_s = jnp.where(lane_seq == s, kd_t, 0.0).astype(BF16)
        sout_ref[s] = d_t[:, hi_r - 1:hi_r] * s_old + _dot(kd_s, v)
    zb_ref[...] = _pack_rows(_head_norm_gate(o_scr[...], gn_ref[...], gg_ref[...].astype(F32)))


def _gla_kernel(*refs, chunk, blocks_per_seq, n_alias):
    (qp, kp, vp, ggp, lap, qs, ks, vs, ggs, las, gn, sin), rest = refs[:12], refs[12 + n_alias:]
    zbp, sp, zbs, sout, tril_ref, code_ref, o_scr, b2_scr, g_scr, qe_scr, kd_scr = rest
    first_block = pl.program_id(0) % blocks_per_seq == 0
    _gla_prompt_body(qp, kp, vp, ggp, lap, gn, zbp, sp, tril_ref, code_ref, b2_scr, g_scr, qe_scr, kd_scr,
                     chunk=chunk, first_block=first_block)
    _gla_sample_body(qs, ks, vs, ggs, las, gn, sin, zbs, sout, o_scr)


def _gla(p, la, gn, state, stacked, layer, offs, batch, seq, chunk, tblk, sb):
    _, nseq, heads, dk, dv = state.shape
    nb = seq // tblk
    steps = batch * heads * nb
    assert steps == (nseq // sb) * heads
    r = sb * SAMPLE_LEN
    rb0 = batch * seq // r
    qb, kb, vb, gb = offs["q"] // dk, offs["k"] // dk, offs["v"] // dv, offs["gg"] // dv
    prow = lambda s: (s // (heads * nb)) * nb + s % nb
    phead = lambda s: (s // nb) % heads
    srow = lambda s: rb0 + s // heads
    shead = lambda s: s % heads
    st_spec = pl.BlockSpec((None, sb, None, dk, dv), lambda s: (layer, s // heads, shead(s), 0, 0))
    in_specs = [pl.BlockSpec((tblk, dk), lambda s: (prow(s), qb + phead(s))),
                pl.BlockSpec((tblk, dk), lambda s: (prow(s), kb + phead(s))),
                pl.BlockSpec((tblk, dv), lambda s: (prow(s), vb + phead(s))),
                pl.BlockSpec((tblk, dv), lambda s: (prow(s), gb + phead(s))),
                pl.BlockSpec((tblk, dk), lambda s: (prow(s), phead(s))),
                pl.BlockSpec((r, dk), lambda s: (srow(s), qb + shead(s))),
                pl.BlockSpec((r, dk), lambda s: (srow(s), kb + shead(s))),
                pl.BlockSpec((r, dv), lambda s: (srow(s), vb + shead(s))),
                pl.BlockSpec((r, dv), lambda s: (srow(s), gb + shead(s))),
                pl.BlockSpec((r, dk), lambda s: (srow(s), shead(s))),
                pl.BlockSpec((None, 1, dv), lambda s: (layer, 0, 0)),
                st_spec]
    args = [p] * 4 + [la] + [p] * 4 + [la, gn.reshape(gn.shape[0], 1, dv), state]
    aliases = {}
    if stacked is not None:
        for arr, out_idx in zip(stacked, (1, 3)):
            in_specs.append(pl.BlockSpec(memory_space=pl.ANY))
            args.append(arr)
            aliases[len(args) - 1] = out_idx
    return pl.pallas_call(
        functools.partial(_gla_kernel, chunk=chunk, blocks_per_seq=nb, n_alias=len(aliases)),
        grid=(steps,),
        in_specs=in_specs,
        out_specs=[pl.BlockSpec((tblk // 2, dv), lambda s: (prow(s), phead(s))),
                   pl.BlockSpec((None, None, None, dk, dv), lambda s: (layer, s // (heads * nb), phead(s), 0, 0)),
                   pl.BlockSpec((r // 2, dv), lambda s: (s // heads, shead(s))),
                   st_spec],
        out_shape=[jax.ShapeDtypeStruct((batch * seq // 2, heads * dv), jnp.uint32),
                   jax.ShapeDtypeStruct((state.shape[0], batch, heads, dk, dv), F32),
                   jax.ShapeDtypeStruct((nseq * SAMPLE_LEN // 2, heads * dv), jnp.uint32),
                   jax.ShapeDtypeStruct(state.shape, F32)],
        scratch_shapes=[pltpu.VMEM((chunk, chunk), BF16), pltpu.VMEM((chunk // 2, chunk // 2), jnp.int32),
                        pltpu.VMEM((r, dv), F32),
                        pltpu.VMEM((chunk, dk), F32), pltpu.VMEM((len(_levels(chunk)), chunk, dk), BF16),
                        pltpu.VMEM((chunk, dk), BF16), pltpu.VMEM((chunk, dk), F32)],
        input_output_aliases=aliases,
        compiler_params=_cparams("arbitrary"),
        name="gla",
    )(*args)


def _short_conv(u, um1, um2, w):
    return um2 * w[0:1, :] + um1 * w[1:2, :] + u * w[2:3, :]


def _inproj_conv_kernel(h_ref, wcb_ref, wcc_ref, wcx_ref, wcg_ref, cw_ref, st_ref, za_ref, nbp_ref, nbs_ref,
                        wb_ref, halo_ref, *, prompt_tiles, tiles_per_seq, sub_m):
    i = pl.program_id(1)
    tc = wcb_ref.shape[0]
    tm = 2 * h_ref.shape[0]

    @pl.when(i == 0)
    def _():
        for g, ref in enumerate((wcb_ref, wcc_ref, wcx_ref, wcg_ref)):
            wb_ref[:, g * tc:(g + 1) * tc] = ref[...].T.astype(BF16)

    w = cw_ref[...]

    def project(r):
        hh = _unpack_rows(h_ref[r * sub_m // 2:(r + 1) * sub_m // 2, :])
        bc = _dot(hh, wb_ref[:, 0:2 * tc])
        xg = _dot(hh, wb_ref[:, 2 * tc:4 * tc])
        cg = xg[:, tc:]
        return bc[:, :tc] * (cg * _sigmoid(cg)), bc[:, tc:] * xg[:, :tc]

    @pl.when(i < prompt_tiles)
    def _():
        @pl.when(i % tiles_per_seq == 0)
        def _():
            halo_ref[...] = jnp.zeros_like(halo_ref)

        for r in range(tm // sub_m):
            gate, u = project(r)
            hr = halo_ref.shape[0]
            prev = halo_ref[...]
            row = lax.broadcasted_iota(jnp.int32, u.shape, 0)
            um1 = jnp.where(row == 0, prev[hr - 1:hr, :], pltpu.roll(u, 1, 0))
            um2 = jnp.where(row == 0, prev[hr - 2:hr - 1, :],
                            jnp.where(row == 1, prev[hr - 1:hr, :], pltpu.roll(u, 2, 0)))
            za_ref[r * sub_m // 2:(r + 1) * sub_m // 2, :] = _pack_rows(gate * _short_conv(u, um1, um2, w))
            halo_ref[...] = u[sub_m - hr:sub_m, :]
        nbp_ref[...] = u[sub_m - (CONV_WIDTH - 1):sub_m, :]

    @pl.when(i >= prompt_tiles)
    def _():
        nsq = sub_m // SAMPLE_LEN
        for r in range(tm // sub_m):
            gate, u = project(r)
            st = st_ref[r * nsq:(r + 1) * nsq]
            rows = lambda x: jnp.broadcast_to(x, (nsq, SAMPLE_LEN, tc)).reshape(sub_m, tc)
            s0, s1 = rows(st[:, 0:1, :]), rows(st[:, 1:2, :])
            pos = lax.broadcasted_iota(jnp.int32, u.shape, 0) & (SAMPLE_LEN - 1)
            um1 = jnp.where(pos == 0, s1, pltpu.roll(u, 1, 0))
            um2 = jnp.where(pos == 0, s0, jnp.where(pos == 1, s1, pltpu.roll(u, 2, 0)))
            za_ref[r * sub_m // 2:(r + 1) * sub_m // 2, :] = _pack_rows(gate * _short_conv(u, um1, um2, w))
            nbs_ref[r * nsq:(r + 1) * nsq] = u.reshape(nsq, SAMPLE_LEN, tc)[:, SAMPLE_LEN - (CONV_WIDTH - 1):, :]


def _inproj_conv(h, w_in_t, conv_w, state, layer, woffs, batch, seq, tm, tc):
    m, d = 2 * h.shape[0], h.shape[1]
    nsamp, cd = state.shape[1], state.shape[3]
    pt = batch * seq // tm
    tiles_per_seq = seq // tm
    sps = tm // SAMPLE_LEN
    sub_m = min(tm, 512)
    kern = functools.partial(_inproj_conv_kernel, prompt_tiles=pt, tiles_per_seq=tiles_per_seq, sub_m=sub_m)
    w_spec = lambda name: pl.BlockSpec((None, tc, d), lambda j, i: (layer, woffs[name] // tc + j, 0))
    samp = lambda i: jnp.maximum(i - pt, 0)
    return pl.pallas_call(
        kern,
        grid=(cd // tc, m // tm),
        in_specs=[pl.BlockSpec((tm // 2, d), lambda j, i: (i, 0)),
                  w_spec("cB"), w_spec("cC"), w_spec("cx"), w_spec("cg"),
                  pl.BlockSpec((None, CONV_WIDTH, tc), lambda j, i: (layer, 0, j)),
                  pl.BlockSpec((None, sps, CONV_WIDTH - 1, tc), lambda j, i: (layer, samp(i), 0, j))],
        out_specs=[pl.BlockSpec((tm // 2, tc), lambda j, i: (i, j)),
                   pl.BlockSpec((None, CONV_WIDTH - 1, tc),
                                lambda j, i: (jnp.minimum(i // tiles_per_seq, batch - 1), 0, j)),
                   pl.BlockSpec((sps, CONV_WIDTH - 1, tc), lambda j, i: (samp(i), 0, j))],
        out_shape=[jax.ShapeDtypeStruct((m // 2, cd), jnp.uint32),
                   jax.ShapeDtypeStruct((batch, CONV_WIDTH - 1, cd), F32),
                   jax.ShapeDtypeStruct((nsamp, CONV_WIDTH - 1, cd), F32)],
        scratch_shapes=[pltpu.VMEM((d, 4 * tc), BF16), pltpu.VMEM((8, tc), F32)],
        compiler_params=_cparams("arbitrary", "arbitrary"),
        name="inproj_conv",
    )(h, w_in_t, w_in_t, w_in_t, w_in_t, conv_w, state)


def _branch_kernel(za_ref, zbp_ref, zbs_ref, wa_ref, wb_ref, sga_ref, sgb_ref, o_ref, wa_bf, wb_bf, *, prompt_tiles):
    i = pl.program_id(1)

    @pl.when(i == 0)
    def _():
        wa_bf[...] = wa_ref[...].astype(BF16)
        wb_bf[...] = wb_ref[...].astype(BF16)

    def merge(zb_ref):
        tn = o_ref.shape[1]
        sub = min(tn, MXU_WIDTH)
        for c in range(tn // sub):
            cols = slice(c * sub, (c + 1) * sub)
            a = _dot(_unpack_rows(za_ref[...]), wa_bf[:, cols])
            b = _dot(_unpack_rows(zb_ref[...]), wb_bf[:, cols])
            o_ref[:, cols] = _pack_rows(sga_ref[:, cols].astype(F32) * a + sgb_ref[:, cols].astype(F32) * b)

    @pl.when(i < prompt_tiles)
    def _():
        merge(zbp_ref)

    @pl.when(i >= prompt_tiles)
    def _():
        merge(zbs_ref)


def _branch(za, zb_p, zb_s, w_a, w_b, p, layer, offs, tm, tn):
    m, cd = 2 * za.shape[0], za.shape[1]
    dvt = zb_p.shape[1]
    d = w_a.shape[2]
    pt = 2 * zb_p.shape[0] // tm
    gao, gbo = offs["ga"] // tn, offs["gb"] // tn
    return pl.pallas_call(
        functools.partial(_branch_kernel, prompt_tiles=pt),
        grid=(d // tn, m // tm),
        in_specs=[pl.BlockSpec((tm // 2, cd), lambda j, i: (i, 0)),
                  pl.BlockSpec((tm // 2, dvt), lambda j, i: (jnp.minimum(i, pt - 1), 0)),
                  pl.BlockSpec((tm // 2, dvt), lambda j, i: (jnp.maximum(i - pt, 0), 0)),
                  pl.BlockSpec((None, cd, tn), lambda j, i: (layer, 0, j)),
                  pl.BlockSpec((None, dvt, tn), lambda j, i: (layer, 0, j)),
                  pl.BlockSpec((tm, tn), lambda j, i: (i, gao + j)),
                  pl.BlockSpec((tm, tn), lambda j, i: (i, gbo + j))],
        out_specs=pl.BlockSpec((tm // 2, tn), lambda j, i: (i, j)),
        out_shape=jax.ShapeDtypeStruct((m // 2, d), jnp.uint32),
        scratch_shapes=[pltpu.VMEM((cd, tn), BF16), pltpu.VMEM((dvt, tn), BF16)],
        compiler_params=_cparams("arbitrary", "arbitrary"),
        name="branch",
    )(za, zb_p, zb_s, w_a, w_b, p, p)


def _out_kernel(m_ref, w_ref, x_ref, g_ref, xo_ref, ho_ref, w_bf):
    @pl.when(pl.program_id(0) == 0)
    def _():
        w_bf[...] = w_ref[...].astype(BF16)

    x = x_ref[...] + _dot(_unpack_rows(m_ref[...]), w_bf[...])
    xo_ref[...] = x
    ho_ref[...] = _pack_rows(_rms(x, g_ref[...]))


def _out_final_kernel(m_ref, w_ref, x_ref, g_ref, yp_ref, ys_ref, w_bf, *, prompt_tiles):
    i = pl.program_id(0)

    @pl.when(i == 0)
    def _():
        w_bf[...] = w_ref[...].astype(BF16)

    y = _rms(x_ref[...] + _dot(_unpack_rows(m_ref[...]), w_bf[...]), g_ref[...])

    @pl.when(i < prompt_tiles)
    def _():
        yp_ref[...] = y

    @pl.when(i >= prompt_tiles)
    def _():
        ys_ref[...] = y


def _out(merged, w_out, x, g, layer, tm, prompt_rows=None):
    m, d = x.shape
    in_specs = [pl.BlockSpec((tm // 2, d), lambda i: (i, 0)),
                pl.BlockSpec((None, d, d), lambda i: (layer, 0, 0), pipeline_mode=pl.Buffered(1)),
                pl.BlockSpec((tm, d), lambda i: (i, 0)),
                pl.BlockSpec((1, d), lambda i: (0, 0))]
    if prompt_rows is None:
        kern = _out_kernel
        out_specs = [pl.BlockSpec((tm, d), lambda i: (i, 0)), pl.BlockSpec((tm // 2, d), lambda i: (i, 0))]
        out_shape = [jax.ShapeDtypeStruct((m, d), F32), jax.ShapeDtypeStruct((m // 2, d), jnp.uint32)]
    else:
        pt = prompt_rows // tm
        kern = functools.partial(_out_final_kernel, prompt_tiles=pt)
        out_specs = [pl.BlockSpec((tm, d), lambda i: (jnp.minimum(i, pt - 1), 0)),
                     pl.BlockSpec((tm, d), lambda i: (jnp.maximum(i - pt, 0), 0))]
        out_shape = [jax.ShapeDtypeStruct((prompt_rows, d), F32), jax.ShapeDtypeStruct((m - prompt_rows, d), F32)]
    return pl.pallas_call(
        kern,
        grid=(m // tm,),
        in_specs=in_specs,
        out_specs=out_specs,
        out_shape=out_shape,
        scratch_shapes=[pltpu.VMEM((d, d), BF16)],
        compiler_params=_cparams("arbitrary"),
        name="outproj",
    )(merged, w_out, x, g.reshape(1, d))


def kernel(x_prompt, x_sample, state_conv, state_gla, norm_g, w_in, conv_w, w_alpha2, b_alpha,
           gla_norm_g, w_branch_a, w_branch_b, w_out, final_norm_g):
    batch, seq, d = x_prompt.shape
    nsamp, slen, _ = x_sample.shape
    depth, _, _, cd = state_conv.shape
    _, _, heads, dk, dv = state_gla.shape
    rank, dkt = w_alpha2.shape[1:]
    dvt = heads * dv
    assert slen == SAMPLE_LEN and dkt == heads * dk and state_conv.shape[2] == CONV_WIDTH - 1
    widths = (("ga", d), ("gb", d), ("cB", cd), ("cC", cd), ("cx", cd), ("cg", cd),
              ("q", dkt), ("k", dkt), ("v", dvt), ("gg", dvt), ("lr", rank))
    woffs, o = {}, 0
    for name, wd in widths:
        woffs[name] = o
        o += wd
    assert o == w_in.shape[2] and rank <= LANES and woffs["lr"] % LANES == 0 and cd == d
    poffs, o = {}, 0
    for name, wd in widths:
        if name in ("ga", "gb", "q", "k", "v", "gg"):
            poffs[name] = o
            o += wd
    poffs["end"] = o

    mp, ms = batch * seq, nsamp * slen
    m = mp + ms
    tm = _tile(ms, 1024)
    tn = _tile(dkt, 1024)
    tm_wide = next((t for t in (1536, 1024) if m % t == 0), tm)
    chunk = _tile(seq, 256)
    sb = _tile(nsamp, 16)
    tblk = seq * batch * sb // nsamp
    assert seq % tblk == 0 and tblk % chunk == 0
    w2 = jnp.pad(w_alpha2, ((0, 0), (0, LANES - rank), (0, 0))).astype(BF16)
    w_in_t = jnp.swapaxes(w_in, 1, 2)

    x, h = _embed(x_prompt.reshape(mp, d), x_sample.reshape(ms, d), norm_g[0])
    conv_p, conv_s, gla_states = [], [], None
    for l in range(depth):
        p = _inproj(h, w_in_t, l, woffs, poffs, dk, tm_wide, tn)
        za, nb_p, nb_s = _inproj_conv(h, w_in_t, conv_w, state_conv, l, woffs, batch, seq, tm, _tile(cd, MXU_WIDTH))
        la = _gate(h, w_in_t, w2, b_alpha, l, woffs["lr"], rank)
        zb_p, gla_p, zb_s, gla_s = _gla(p, la, gla_norm_g, state_gla, gla_states, l, poffs, batch, seq, chunk,
                                        tblk, sb)
        gla_states = (gla_p, gla_s)
        merged = _branch(za, zb_p, zb_s, w_branch_a, w_branch_b, p, l, poffs, tm, _tile(d, 512))
        conv_p.append(nb_p)
        conv_s.append(nb_s)
        if l < depth - 1:
            x, h = _out(merged, w_out, x, norm_g[l + 1], l, _tile(ms, 512))
        else:
            y_p, y_s = _out(merged, w_out, x, final_norm_g, l, _tile(ms, 512), prompt_rows=mp)
    return (y_p.reshape(batch, seq, d), y_s.reshape(nsamp, slen, d),
            jnp.stack(conv_p), gla_p, jnp.stack(conv_s), gla_s)
```
